```python
import jax, jax.numpy as jnp
from jax import lax
import numpy as np

D_MODEL = 1024
BATCH = 16
SEQ = 2048
DEPTH = 2

N_META = 16
EPS = 1e-6
SSM_D_INNER = 2 * D_MODEL
SSM_HEAD_DIM = 64
SSM_HEADS = SSM_D_INNER // SSM_HEAD_DIM
SSM_GROUPS = 4
SSM_STATE = 128
SSM_CONV = 4
SSM_CHUNK = 128
SSM_CONV_DIM = SSM_D_INNER + 2 * SSM_GROUPS * SSM_STATE
MLA_HEADS = 8
MLA_Q_LORA = D_MODEL // 2
MLA_KV_LORA = D_MODEL // 4
MLA_NOPE = 128
MLA_ROPE = 64
MLA_V = 128
ROPE_THETA = 10000.0
Q_BLOCK = 128
D_FF = 4 * D_MODEL
IN_SPLITS = [SSM_D_INNER, SSM_CONV_DIM, SSM_HEADS, MLA_Q_LORA, MLA_KV_LORA, MLA_ROPE, D_MODEL, D_MODEL]
IN_DIM = sum(IN_SPLITS)

kernel_name = "hybrid_ssd_mla_meta_block"


def rms_norm(x, w):
    xf = x.astype(jnp.float32)
    y = xf * lax.rsqrt(jnp.mean(xf * xf, axis=-1, keepdims=True) + EPS)
    return (y * w.astype(jnp.float32)).astype(x.dtype)


def rope_tables(n_pos, dim):
    inv = ROPE_THETA ** (-jnp.arange(0, dim, 2, dtype=jnp.float32) / dim)
    ang = jnp.arange(n_pos, dtype=jnp.float32)[:, None] * inv[None, :]
    return jnp.cos(ang), jnp.sin(ang)


def apply_rope(x, cos, sin):
    x1, x2 = jnp.split(x.astype(jnp.float32), 2, axis=-1)
    return jnp.concatenate([x1 * cos - x2 * sin, x2 * cos + x1 * sin], axis=-1).astype(x.dtype)


def causal_dwconv(x, w, b):
    k, c = w.shape
    y = lax.conv_general_dilated(x, w[:, None, :].astype(x.dtype), window_strides=(1,),
                                 padding=[(k - 1, 0)], dimension_numbers=("NWC", "WIO", "NWC"),
                                 feature_group_count=c)
    return y + b.astype(x.dtype)


def ssd_chunked(xdt, adt, bm, cm):
    b, t, h, p = xdt.shape
    g, n = bm.shape[-2:]
    e = h // g
    q = SSM_CHUNK
    c = t // q
    xc = xdt.reshape(b, c, q, g, e, p)
    a = adt.astype(jnp.float32).reshape(b, c, q, g, e).transpose(0, 3, 4, 1, 2)
    bc = bm.reshape(b, c, q, g, n)
    cc = cm.reshape(b, c, q, g, n)
    a_cs = jnp.cumsum(a, axis=-1)
    causal = np.tril(np.ones((q, q), dtype=bool))
    l_dec = jnp.exp(jnp.where(causal, a_cs[..., :, None] - a_cs[..., None, :], -jnp.inf))
    cb = jnp.einsum("bclgn,bcsgn->bcgls", cc, bc)
    y_diag = jnp.einsum("bcgls,bgecls,bcsgep->bclgep", cb, l_dec, xc)
    decay_states = jnp.exp(a_cs[..., -1:] - a_cs)
    states = jnp.einsum("bcsgn,bgecs,bcsgep->bcgepn", bc, decay_states, xc)
    chunk_decay = jnp.exp(a_cs[..., -1])

    def step(hs, inp):
        s_c, d_c = inp
        return hs * d_c[..., None, None] + s_c, hs

    h0 = jnp.zeros((b, g, e, p, n), jnp.float32)
    _, prev = lax.scan(step, h0, (states.astype(jnp.float32).transpose(1, 0, 2, 3, 4, 5),
                                  chunk_decay.transpose(3, 0, 1, 2)))
    prev = prev.transpose(1, 0, 2, 3, 4, 5)
    y_off = jnp.einsum("bclgn,bcgepn,bgecl->bclgep", cc, prev, jnp.exp(a_cs))
    return (y_diag + y_off).reshape(b, t, h, p).astype(xdt.dtype)


def ssd_mixer(z, xbc, dt, conv_w, conv_b, dt_bias, a_log, d_skip, norm_w):
    bsz, L, _ = xbc.shape
    xbc = jax.nn.silu(causal_dwconv(xbc, conv_w, conv_b))
    xs, bm, cm = jnp.split(xbc, [SSM_D_INNER, SSM_D_INNER + SSM_GROUPS * SSM_STATE], axis=-1)
    xs = xs.reshape(bsz, L, SSM_HEADS, SSM_HEAD_DIM)
    bm = bm.reshape(bsz, L, SSM_GROUPS, SSM_STATE)
    cm = cm.reshape(bsz, L, SSM_GROUPS, SSM_STATE)
    dt = jax.nn.softplus(dt.astype(jnp.float32) + dt_bias.astype(jnp.float32))
    a = -jnp.exp(a_log.astype(jnp.float32))
    pad = SSM_CHUNK - N_META

    def lpad(t):
        return jnp.pad(t, [(0, 0), (pad, 0)] + [(0, 0)] * (t.ndim - 2))

    y = ssd_chunked(lpad(xs * dt[..., None]), lpad(dt * a), lpad(bm), lpad(cm))[:, pad:]
    y = y + xs * d_skip[:, None].astype(xs.dtype)
    y = y.reshape(bsz, L, SSM_D_INNER) * jax.nn.silu(z)
    gsz = SSM_D_INNER // SSM_GROUPS
    y = rms_norm(y.reshape(bsz, L, SSM_GROUPS, gsz), norm_w.reshape(SSM_GROUPS, gsz))
    return y.reshape(bsz, L, SSM_D_INNER)


def mla_mixer(c_q, c_kv, k_rope, q_norm_w, kv_norm_w, w_uq, w_ukv, cos, sin):
    bsz, L, _ = c_q.shape
    qf = (rms_norm(c_q, q_norm_w) @ w_uq).reshape(bsz, L, MLA_HEADS, MLA_NOPE + MLA_ROPE)
    q_nope, q_pe = jnp.split(qf, [MLA_NOPE], axis=-1)
    q_pe = apply_rope(q_pe, cos[:, None, :], sin[:, None, :])
    kv = (rms_norm(c_kv, kv_norm_w) @ w_ukv).reshape(bsz, L, MLA_HEADS, MLA_NOPE + MLA_V)
    k_nope, v = jnp.split(kv, [MLA_NOPE], axis=-1)
    k_pe = apply_rope(k_rope, cos, sin)
    scale = (MLA_NOPE + MLA_ROPE) ** -0.5
    bounds = [0, N_META] + list(range(N_META + Q_BLOCK, L + 1, Q_BLOCK))
    outs = []
    for qs, qe in zip(bounds[:-1], bounds[1:]):
        s = (jnp.einsum("bqhd,bkhd->bhqk", q_nope[:, qs:qe], k_nope[:, :qe])
             + jnp.einsum("bqhr,bkr->bhqk", q_pe[:, qs:qe], k_pe[:, :qe])).astype(jnp.float32) * scale
        mask = np.arange(qs, qe)[:, None] >= np.arange(qe)[None, :]
        p = jax.nn.softmax(jnp.where(mask, s, -jnp.inf), axis=-1).astype(v.dtype)
        outs.append(jnp.einsum("bhqk,bkhv->bqhv", p, v[:, :qe]))
    o = jnp.concatenate(outs, axis=1)
    return o.reshape(bsz, L, MLA_HEADS * MLA_V)


def setup_inputs(seed: int = 0) -> dict:
    key = jax.random.key(seed)
    ks = jax.random.split(key, 24)
    f32 = jnp.float32
    nrm = lambda k, shape, s: jax.random.normal(k, shape, f32) * s
    gain = lambda k, shape: 1.0 + 0.02 * jax.random.normal(k, shape, f32)
    res_scale = (2 * DEPTH) ** -0.5
    dt0 = jnp.exp(jax.random.uniform(ks[5], (DEPTH, SSM_HEADS), f32, np.log(1e-3), np.log(1e-1)))
    dt_bias = dt0 + jnp.log(-jnp.expm1(-dt0))
    return {
        "x": nrm(ks[0], (BATCH, SEQ, D_MODEL), 1.0),
        "meta_tokens": nrm(ks[1], (N_META, D_MODEL), 1.0),
        "norm_mix_w": gain(ks[2], (DEPTH, D_MODEL)),
        "w_in": nrm(ks[3], (DEPTH, D_MODEL, IN_DIM), D_MODEL ** -0.5),
        "conv_w": nrm(ks[4], (DEPTH, SSM_CONV, SSM_CONV_DIM), SSM_CONV ** -0.5),
        "conv_b": nrm(ks[6], (DEPTH, SSM_CONV_DIM), 0.01),
        "dt_bias": dt_bias,
        "a_log": jnp.log(jax.random.uniform(ks[7], (DEPTH, SSM_HEADS), f32, 1.0, 16.0)),
        "d_skip": 1.0 + 0.1 * jax.random.normal(ks[8], (DEPTH, SSM_HEADS), f32),
        "ssm_norm_w": gain(ks[9], (DEPTH, SSM_D_INNER)),
        "q_norm_w": gain(ks[10], (DEPTH, MLA_Q_LORA)),
        "kv_norm_w": gain(ks[11], (DEPTH, MLA_KV_LORA)),
        "w_uq": nrm(ks[12], (DEPTH, MLA_Q_LORA, MLA_HEADS * (MLA_NOPE + MLA_ROPE)), MLA_Q_LORA ** -0.5),
        "w_ukv": nrm(ks[13], (DEPTH, MLA_KV_LORA, MLA_HEADS * (MLA_NOPE + MLA_V)), MLA_KV_LORA ** -0.5),
        "w_branch_ssm": nrm(ks[14], (DEPTH, SSM_D_INNER, D_MODEL), SSM_D_INNER ** -0.5),
        "w_branch_mla": nrm(ks[15], (DEPTH, MLA_HEADS * MLA_V, D_MODEL), (MLA_HEADS * MLA_V) ** -0.5),
        "w_out": nrm(ks[16], (DEPTH, D_MODEL, D_MODEL), D_MODEL ** -0.5 * res_scale),
        "norm_mlp_w": gain(ks[17], (DEPTH, D_MODEL)),
        "w_mlp_up": nrm(ks[18], (DEPTH, D_MODEL, D_FF), D_MODEL ** -0.5),
        "w_mlp_down": nrm(ks[19], (DEPTH, D_FF, D_MODEL), D_FF ** -0.5 * res_scale),
        "final_norm_w": gain(ks[20], (D_MODEL,)),
    }


def reference(x, meta_tokens, norm_mix_w, w_in, conv_w, conv_b, dt_bias, a_log, d_skip, ssm_norm_w,
              q_norm_w, kv_norm_w, w_uq, w_ukv, w_branch_ssm, w_branch_mla, w_out, norm_mlp_w,
              w_mlp_up, w_mlp_down, final_norm_w):
    bsz = x.shape[0]
    meta = jnp.broadcast_to(meta_tokens.astype(x.dtype)[None], (bsz, N_META, D_MODEL))
    h = jnp.concatenate([meta, x], axis=1)
    L = h.shape[1]
    cos, sin = rope_tables(L, MLA_ROPE)
    split_idx = np.cumsum(IN_SPLITS)[:-1].tolist()
    for i in range(DEPTH):
        u = rms_norm(h, norm_mix_w[i])
        z, xbc, dt, c_q, c_kv, k_rope, g_ssm, g_mla = jnp.split(u @ w_in[i], split_idx, axis=-1)
        y_ssm = ssd_mixer(z, xbc, dt, conv_w[i], conv_b[i], dt_bias[i], a_log[i], d_skip[i], ssm_norm_w[i])
        y_mla = mla_mixer(c_q, c_kv, k_rope, q_norm_w[i], kv_norm_w[i], w_uq[i], w_ukv[i], cos, sin)
        mixed = (jax.nn.sigmoid(g_ssm) * (y_ssm @ w_branch_ssm[i])
                 + jax.nn.sigmoid(g_mla) * (y_mla @ w_branch_mla[i]))
        h = h + mixed @ w_out[i]
        v = rms_norm(h, norm_mlp_w[i])
        h = h + jnp.square(jax.nn.relu(v @ w_mlp_up[i])) @ w_mlp_down[i]
    return rms_norm(h, final_norm_w)[:, N_META:]
```

```python
import functools

import jax
import jax.numpy as jnp
import numpy as np
from jax import lax
from jax.experimental import pallas as pl
from jax.experimental.pallas import tpu as pltpu

F32 = jnp.float32
BF16 = jnp.bfloat16

N_META = 16
EPS = 1e-6
CHUNK = 128
META_PAD = CHUNK - N_META
SSM_HEAD_DIM = 64
SSM_GROUPS = 4
SSM_STATE = 128
SSM_CONV = 4
MLA_HEADS = 8
MLA_NOPE = 128
MLA_ROPE = 64
MLA_V = 128
ROPE_THETA = 10000.0
QK_SLAB = 256
MASK_NEG = -1e30
V7X_VMEM_LIMIT = 56 * 1024 * 1024


def _rms(x, w):
    var = jnp.mean(x * x, axis=-1, keepdims=True)
    return x * lax.rsqrt(var + EPS) * w


def _sigmoid(x):
    return 1.0 / (1.0 + jnp.exp(-x))


def _split3(x):
    hi = x.astype(BF16)
    r1 = x - hi.astype(F32)
    mid = r1.astype(BF16)
    lo = (r1 - mid.astype(F32)).astype(BF16)
    return hi, mid, lo


def _dot(a, b):
    return jnp.dot(a, b, preferred_element_type=F32)


def _dot_nt(a, b):
    return lax.dot_general(a, b, (((1,), (1,)), ((), ())), preferred_element_type=F32)


def _dot_tn(a, b):
    return lax.dot_general(a, b, (((0,), (0,)), ((), ())), preferred_element_type=F32)


def _in_proj_body(h_ref, nw_ref, w_ref, ws_ref, o_ref, os_ref, u_ref):
    @pl.when(pl.program_id(1) == 0)
    def _():
        u = _rms(h_ref[...], nw_ref[...]).astype(BF16)
        u_ref[...] = u
        os_ref[...] = _dot(u, ws_ref[...])

    o_ref[...] = _dot(u_ref[...], w_ref[...]).astype(o_ref.dtype)


def _in_proj(h, nw, w_main, w_small, tm, tn):
    m, d = h.shape
    n = w_main.shape[1]
    ns = w_small.shape[1]
    return pl.pallas_call(
        _in_proj_body,
        grid=(m // tm, n // tn),
        in_specs=[
            pl.BlockSpec((tm, d), lambda i, j: (i, 0)),
            pl.BlockSpec((1, d), lambda i, j: (0, 0)),
            pl.BlockSpec((d, tn), lambda i, j: (0, j)),
            pl.BlockSpec((d, ns), lambda i, j: (0, 0)),
        ],
        out_specs=[
            pl.BlockSpec((tm, tn), lambda i, j: (i, j)),
            pl.BlockSpec((tm, ns), lambda i, j: (i, 0)),
        ],
        out_shape=[
            jax.ShapeDtypeStruct((m, n), BF16),
            jax.ShapeDtypeStruct((m, ns), F32),
        ],
        scratch_shapes=[pltpu.VMEM((tm, d), BF16)],
        compiler_params=pltpu.CompilerParams(
            dimension_semantics=("parallel", "arbitrary"),
            vmem_limit_bytes=V7X_VMEM_LIMIT),
        name="in_proj",
    )(h, nw, w_main, w_small)


def _conv_silu(x, tail8, w, b):
    row8 = lax.broadcasted_iota(jnp.int32, (8, 1), 0)
    x_top = x[0:8]
    acc = b + w[SSM_CONV - 1:SSM_CONV] * x
    acc_top = b + w[SSM_CONV - 1:SSM_CONV] * x_top
    for s in range(1, SSM_CONV):
        wk = w[SSM_CONV - 1 - s:SSM_CONV - s]
        acc = acc + wk * pltpu.roll(x, s, 0)
        top = jnp.where(row8 < s, pltpu.roll(tail8, s, 0), pltpu.roll(x_top, s, 0))
        acc_top = acc_top + wk * top
    y = jnp.concatenate([acc_top, acc[8:]], axis=0)
    return y * _sigmoid(y)


def _ssd_body(z_ref, xs_ref, b_ref, c_ref, dt_ref, cw_ref, cb_ref, dtb_ref, alog_ref,
              dskip_ref, nw_ref, expand_ref, y_ref, state_ref, tail_ref):
    c = pl.program_id(1)
    d_inner = xs_ref.shape[1]
    gn = b_ref.shape[1]
    gw = d_inner // SSM_GROUPS
    heads_per_group = gw // SSM_HEAD_DIM

    @pl.when(c == 0)
    def _():
        state_ref[...] = jnp.zeros_like(state_ref)
        tail_ref[...] = jnp.zeros_like(tail_ref)

    x_raw = xs_ref[...].astype(F32)
    b_raw = b_ref[...].astype(F32)
    c_raw = c_ref[...].astype(F32)
    xs = _conv_silu(x_raw, tail_ref[:, 0:d_inner], cw_ref[:, 0:d_inner], cb_ref[:, 0:d_inner])
    bm = _conv_silu(b_raw, tail_ref[:, d_inner:d_inner + gn],
                    cw_ref[:, d_inner:d_inner + gn], cb_ref[:, d_inner:d_inner + gn])
    cm = _conv_silu(c_raw, tail_ref[:, d_inner + gn:], cw_ref[:, d_inner + gn:], cb_ref[:, d_inner + gn:])
    tail_ref[:, 0:d_inner] = x_raw[CHUNK - 8:]
    tail_ref[:, d_inner:d_inner + gn] = b_raw[CHUNK - 8:]
    tail_ref[:, d_inner + gn:] = c_raw[CHUNK - 8:]

    row = lax.broadcasted_iota(jnp.int32, (CHUNK, 1), 0)
    xdt = dt_ref[...] + dtb_ref[...]
    dt = jnp.maximum(xdt, 0.0) + jnp.log(1.0 + jnp.exp(-jnp.abs(xdt)))
    dt = jnp.where((c > 0) | (row >= META_PAD), dt, 0.0)
    adt = dt * (-jnp.exp(alog_ref[...]))

    ri = lax.broadcasted_iota(jnp.int32, (CHUNK, CHUNK), 0)
    ci = lax.broadcasted_iota(jnp.int32, (CHUNK, CHUNK), 1)
    causal = ri >= ci
    tri = causal.astype(BF16)
    hi, mid, lo = _split3(adt)
    a_cs = _dot(tri, hi) + _dot(tri, mid) + _dot(tri, lo)
    a_cs_t = a_cs.T
    dt_t = dt.T
    exp_a = jnp.exp(a_cs)
    w_state = jnp.exp(a_cs[CHUNK - 1:CHUNK] - a_cs) * dt

    parts = [jnp.concatenate([p, q], axis=0) for p, q in zip(_split3(w_state), _split3(exp_a))]
    expanded = _dot(jnp.concatenate(parts, axis=1), expand_ref[...])
    w_exp = expanded[0:CHUNK]
    exp_a_exp = expanded[CHUNK:]

    xs_b = xs.astype(BF16)
    xw = (xs * w_exp).astype(BF16)
    lane = lax.broadcasted_iota(jnp.int32, (1, 2 * SSM_HEAD_DIM), 1)
    lo_half = lane < SSM_HEAD_DIM

    for g in range(SSM_GROUPS):
        gs = slice(g * gw, (g + 1) * gw)
        bm_g = bm[:, g * SSM_STATE:(g + 1) * SSM_STATE].astype(BF16)
        cm_g = cm[:, g * SSM_STATE:(g + 1) * SSM_STATE].astype(BF16)
        cb = _dot_nt(cm_g, bm_g)
        y_pairs = []
        for pj in range(heads_per_group // 2):
            ms = []
            for e in range(2):
                hd = g * heads_per_group + 2 * pj + e
                seg = a_cs[:, hd:hd + 1] - a_cs_t[hd:hd + 1, :]
                decay = jnp.exp(jnp.where(causal, seg, MASK_NEG))
                ms.append((cb * decay * dt_t[hd:hd + 1, :]).astype(BF16))
            lhs = jnp.concatenate(ms, axis=1)
            p0 = g * gw + pj * 2 * SSM_HEAD_DIM
            xpair = xs_b[:, p0:p0 + 2 * SSM_HEAD_DIM]
            zero = jnp.zeros_like(xpair)
            rhs = jnp.concatenate([jnp.where(lo_half, xpair, zero),
                                   jnp.where(lo_half, zero, xpair)], axis=0)
            y_pairs.append(_dot(lhs, rhs))
        y_diag = jnp.concatenate(y_pairs, axis=1)
        prev = state_ref[:, gs]
        y_off = _dot(cm_g, prev.astype(BF16)) * exp_a_exp[:, gs]
        state_ref[:, gs] = prev * exp_a_exp[CHUNK - 1:CHUNK, gs] + _dot_tn(bm_g, xw[:, gs])
        y = y_diag + y_off + xs[:, gs] * dskip_ref[:, gs]
        zg = z_ref[:, gs].astype(F32)
        y = y * (zg * _sigmoid(zg))
        y_ref[:, gs] = _rms(y, nw_ref[:, gs]).astype(y_ref.dtype)


def _ssd(proj, small, conv_w, conv_b, dt_bias, a_log, d_skip, norm_w, expand, *, batch, seq, d_inner):
    m = proj.shape[0]
    n_tok_chunks = seq // CHUNK
    nc = n_tok_chunks + 1
    gn = SSM_GROUPS * SSM_STATE
    conv_dim = d_inner + 2 * gn
    tok_chunks_total = batch * n_tok_chunks

    def rowblk(b, c):
        return jnp.where(c == 0, tok_chunks_total + b, b * n_tok_chunks + c - 1)

    def col(width_blk):
        return lambda b, c: (rowblk(b, c), width_blk)

    const = lambda b, c: (0, 0)
    return pl.pallas_call(
        _ssd_body,
        grid=(batch, nc),
        in_specs=[
            pl.BlockSpec((CHUNK, d_inner), col(0)),
            pl.BlockSpec((CHUNK, d_inner), col(1)),
            pl.BlockSpec((CHUNK, gn), col(2 * d_inner // gn)),
            pl.BlockSpec((CHUNK, gn), col(2 * d_inner // gn + 1)),
            pl.BlockSpec((CHUNK, 128), col(1)),
            pl.BlockSpec((SSM_CONV, conv_dim), const),
            pl.BlockSpec((1, conv_dim), const),
            pl.BlockSpec((1, 128), const),
            pl.BlockSpec((1, 128), const),
            pl.BlockSpec((1, d_inner), const),
            pl.BlockSpec((1, d_inner), const),
            pl.BlockSpec(expand.shape, const),
        ],
        out_specs=pl.BlockSpec((CHUNK, d_inner), col(0)),
        out_shape=jax.ShapeDtypeStruct((m, d_inner), BF16),
        scratch_shapes=[pltpu.VMEM((SSM_STATE, d_inner), F32),
                        pltpu.VMEM((8, conv_dim), F32)],
        compiler_params=pltpu.CompilerParams(
            dimension_semantics=("parallel", "arbitrary"),
            vmem_limit_bytes=V7X_VMEM_LIMIT),
        name="ssd",
    )(proj, proj, proj, proj, small, conv_w, conv_b, dt_bias, a_log, d_skip, norm_w, expand)


def _rope(x, cos_t, sin_t):
    return x * cos_t + pltpu.roll(x, 64, 1) * sin_t


def _mla_proj_body(cq_ref, ckv_ref, kr_ref, qnw_ref, kvnw_ref, wq_ref, wkv_ref, cos_ref, sin_ref,
                   q_ref, k_ref, v_ref, *, scale):
    cos_t = cos_ref[...]
    sin_t = sin_ref[...]
    cqn = _rms(cq_ref[...].astype(F32), qnw_ref[...]).astype(BF16)
    qf = _dot(cqn, wq_ref[...])
    ckvn = _rms(ckv_ref[...].astype(F32), kvnw_ref[...]).astype(BF16)
    kvf = _dot(ckvn, wkv_ref[...])
    k_pe = _rope(kr_ref[...], cos_t, sin_t).astype(BF16)
    for hd in range(MLA_HEADS):
        o = hd * QK_SLAB
        q_ref[:, o:o + MLA_NOPE] = (qf[:, o:o + MLA_NOPE] * scale).astype(BF16)
        q_ref[:, o + MLA_NOPE:o + QK_SLAB] = (
            _rope(qf[:, o + MLA_NOPE:o + QK_SLAB], cos_t, sin_t) * scale).astype(BF16)
        k_ref[:, o:o + MLA_NOPE] = kvf[:, hd * MLA_NOPE:(hd + 1) * MLA_NOPE].astype(BF16)
        k_ref[:, o + MLA_NOPE:o + QK_SLAB] = k_pe
    v_ref[...] = kvf[:, MLA_HEADS * MLA_NOPE:].astype(BF16)


def _mla_proj(proj, small, q_norm_w, kv_norm_w, wq, wkv, cos_tab, sin_tab, *, tm, seq, n_tok,
              cq_col, ckv_col, q_lora, kv_lora):
    m = proj.shape[0]
    n_tok_tiles = n_tok // tm
    tiles_per_seq = seq // tm

    def tab(i):
        return (jnp.where(i < n_tok_tiles, i % tiles_per_seq, tiles_per_seq), 0)

    const = lambda i: (0, 0)
    scale = float((MLA_NOPE + MLA_ROPE) ** -0.5)
    return pl.pallas_call(
        functools.partial(_mla_proj_body, scale=scale),
        grid=(m // tm,),
        in_specs=[
            pl.BlockSpec((tm, q_lora), lambda i: (i, cq_col // q_lora)),
            pl.BlockSpec((tm, kv_lora), lambda i: (i, ckv_col // kv_lora)),
            pl.BlockSpec((tm, 128), lambda i: (i, 0)),
            pl.BlockSpec((1, q_lora), const),
            pl.BlockSpec((1, kv_lora), const),
            pl.BlockSpec(wq.shape, const),
            pl.BlockSpec(wkv.shape, const),
            pl.BlockSpec((tm, 128), tab),
            pl.BlockSpec((tm, 128), tab),
        ],
        out_specs=[
            pl.BlockSpec((tm, MLA_HEADS * QK_SLAB), lambda i: (i, 0)),
            pl.BlockSpec((tm, MLA_HEADS * QK_SLAB), lambda i: (i, 0)),
            pl.BlockSpec((tm, MLA_HEADS * MLA_V), lambda i: (i, 0)),
        ],
        out_shape=[
            jax.ShapeDtypeStruct((m, MLA_HEADS * QK_SLAB), BF16),
            jax.ShapeDtypeStruct((m, MLA_HEADS * QK_SLAB), BF16),
            jax.ShapeDtypeStruct((m, MLA_HEADS * MLA_V), BF16),
        ],
        compiler_params=pltpu.CompilerParams(
            dimension_semantics=("parallel",),
            vmem_limit_bytes=V7X_VMEM_LIMIT),
        name="mla_proj",
    )(proj, proj, small, q_norm_w, kv_norm_w, wq, wkv, cos_tab, sin_tab)


def _attend(q, kk_ref, vv_ref, n_left, tq, diag_mask, pad_ok):
    s_d = jnp.where(diag_mask, _dot_nt(q, kk_ref[n_left:n_left + tq, :]), MASK_NEG)
    m = jnp.max(s_d, axis=-1, keepdims=True)
    if n_left:
        s_l = _dot_nt(q, kk_ref[0:n_left, :])
        s_meta = jnp.where(pad_ok, s_l[:, 0:CHUNK], MASK_NEG)
        m = jnp.maximum(m, jnp.max(s_meta, axis=-1, keepdims=True))
        if n_left > CHUNK:
            s_rest = s_l[:, CHUNK:]
            m = jnp.maximum(m, jnp.max(s_rest, axis=-1, keepdims=True))
    p_d = jnp.exp(s_d - m)
    denom = jnp.sum(p_d, axis=-1, keepdims=True)
    o = _dot(p_d.astype(BF16), vv_ref[n_left:n_left + tq, :])
    if n_left:
        p_meta = jnp.exp(s_meta - m)
        denom = denom + jnp.sum(p_meta, axis=-1, keepdims=True)
        o = o + _dot(p_meta.astype(BF16), vv_ref[0:CHUNK, :])
        if n_left > CHUNK:
            p_rest = jnp.exp(s_rest - m)
            denom = denom + jnp.sum(p_rest, axis=-1, keepdims=True)
            o = o + _dot(p_rest.astype(BF16), vv_ref[CHUNK:n_left, :])
    return o / denom


def _attn_body(qt_ref, qm_ref, kt_ref, km_ref, vt_ref, vm_ref, ot_ref, om_ref, kk_ref, vv_ref, *, tq):
    seq = qt_ref.shape[0]
    kk_ref[0:CHUNK, :] = km_ref[...]
    kk_ref[CHUNK:, :] = kt_ref[...]
    vv_ref[0:CHUNK, :] = vm_ref[...]
    vv_ref[CHUNK:, :] = vt_ref[...]

    key = lax.broadcasted_iota(jnp.int32, (1, CHUNK), 1)
    pad_ok = key >= META_PAD
    rm = lax.broadcasted_iota(jnp.int32, (CHUNK, CHUNK), 0)
    cm = lax.broadcasted_iota(jnp.int32, (CHUNK, CHUNK), 1)
    meta_mask = (rm >= cm) & (cm >= META_PAD)
    om_ref[...] = _attend(qm_ref[...], kk_ref, vv_ref, 0, CHUNK, meta_mask, pad_ok).astype(om_ref.dtype)

    rq = lax.broadcasted_iota(jnp.int32, (tq, tq), 0)
    cq = lax.broadcasted_iota(jnp.int32, (tq, tq), 1)
    tri = rq >= cq
    for qs in range(0, seq, tq):
        o = _attend(qt_ref[qs:qs + tq, :], kk_ref, vv_ref, CHUNK + qs, tq, tri, pad_ok)
        ot_ref[qs:qs + tq, :] = o.astype(ot_ref.dtype)


def _attention(q, k, v, *, batch, seq, n_tok, tq):
    lp = seq + CHUNK
    meta0 = n_tok // CHUNK
    tok_q = lambda b, h: (b, h)
    meta_q = lambda b, h: (meta0 + b, h)
    return pl.pallas_call(
        functools.partial(_attn_body, tq=tq),
        grid=(batch, MLA_HEADS),
        in_specs=[
            pl.BlockSpec((seq, QK_SLAB), tok_q),
            pl.BlockSpec((CHUNK, QK_SLAB), meta_q),
            pl.BlockSpec((seq, QK_SLAB), tok_q),
            pl.BlockSpec((CHUNK, QK_SLAB), meta_q),
            pl.BlockSpec((seq, MLA_V), tok_q),
            pl.BlockSpec((CHUNK, MLA_V), meta_q),
        ],
        out_specs=[
            pl.BlockSpec((seq, MLA_V), tok_q),
            pl.BlockSpec((CHUNK, MLA_V), lambda b, h: (b, h)),
        ],
        out_shape=[
            jax.ShapeDtypeStruct((n_tok, MLA_HEADS * MLA_V), BF16),
            jax.ShapeDtypeStruct((batch * CHUNK, MLA_HEADS * MLA_V), BF16),
        ],
        scratch_shapes=[pltpu.VMEM((lp, QK_SLAB), BF16), pltpu.VMEM((lp, MLA_V), BF16)],
        compiler_params=pltpu.CompilerParams(
            dimension_semantics=("parallel", "parallel"),
            vmem_limit_bytes=V7X_VMEM_LIMIT),
        name="mla_attention",
    )(q, q, k, k, v, v)


def _mix_body(ys_ref, ymt_ref, ymm_ref, gs_ref, gm_ref, h_ref, wbs_ref, wbm_ref, wo_ref, o_ref,
              *, n_tok_tiles):
    i = pl.program_id(0)
    tm = h_ref.shape[0]
    is_meta = i >= n_tok_tiles
    ym = jnp.where(is_meta, ymm_ref[...], ymt_ref[...])
    a = _dot(ys_ref[...], wbs_ref[...])
    b = _dot(ym, wbm_ref[...])
    mixed = _sigmoid(gs_ref[...].astype(F32)) * a + _sigmoid(gm_ref[...].astype(F32)) * b
    hn = h_ref[...] + _dot(mixed.astype(BF16), wo_ref[...])
    row = lax.broadcasted_iota(jnp.int32, (tm, 1), 0)
    inert = is_meta & ((row & (CHUNK - 1)) < META_PAD)
    o_ref[...] = jnp.where(inert, 0.0, hn)


def _mix(y_ssm, y_mla_tok, y_mla_meta, proj, h, w_bs, w_bm, w_o, *, tm, n_tok, gs_col, gm_col):
    m, d = h.shape
    n_tok_tiles = n_tok // tm
    n_meta_tiles = (m - n_tok) // tm
    const = lambda i: (0, 0)
    single = pl.Buffered(1)
    return pl.pallas_call(
        functools.partial(_mix_body, n_tok_tiles=n_tok_tiles),
        grid=(m // tm,),
        in_specs=[
            pl.BlockSpec((tm, y_ssm.shape[1]), lambda i: (i, 0)),
            pl.BlockSpec((tm, d), lambda i: (jnp.minimum(i, n_tok_tiles - 1), 0)),
            pl.BlockSpec((tm, d), lambda i: (jnp.clip(i - n_tok_tiles, 0, n_meta_tiles - 1), 0)),
            pl.BlockSpec((tm, d), lambda i: (i, gs_col // d)),
            pl.BlockSpec((tm, d), lambda i: (i, gm_col // d)),
            pl.BlockSpec((tm, d), lambda i: (i, 0)),
            pl.BlockSpec(w_bs.shape, const, pipeline_mode=single),
            pl.BlockSpec(w_bm.shape, const, pipeline_mode=single),
            pl.BlockSpec(w_o.shape, const, pipeline_mode=single),
        ],
        out_specs=pl.BlockSpec((tm, d), lambda i: (i, 0)),
        out_shape=jax.ShapeDtypeStruct((m, d), F32),
        compiler_params=pltpu.CompilerParams(
            dimension_semantics=("parallel",),
            vmem_limit_bytes=V7X_VMEM_LIMIT),
        name="mix_out",
    )(y_ssm, y_mla_tok, y_mla_meta, proj, proj, h, w_bs, w_bm, w_o)


def _mlp_body(h_ref, nw_ref, wu_ref, wd_ref, fnw_ref, o_ref, *, ff_chunk, final):
    h = h_ref[...]
    v = _rms(h, nw_ref[...]).astype(BF16)
    d_ff = wu_ref.shape[1]
    acc = jnp.zeros_like(h)
    for f0 in range(0, d_ff, ff_chunk):
        a = _dot(v, wu_ref[:, f0:f0 + ff_chunk])
        a = jnp.square(jnp.maximum(a, 0.0)).astype(BF16)
        acc = acc + _dot(a, wd_ref[f0:f0 + ff_chunk, :])
    hn = h + acc
    if final:
        hn = _rms(hn, fnw_ref[...])
    o_ref[...] = hn


def _mlp(h, nw, w_up, w_down, final_nw, *, tm, rows, final):
    d = h.shape[1]
    const = lambda i: (0, 0)
    single = pl.Buffered(1)
    return pl.pallas_call(
        functools.partial(_mlp_body, ff_chunk=1024, final=final),
        grid=(rows // tm,),
        in_specs=[
            pl.BlockSpec((tm, d), lambda i: (i, 0)),
            pl.BlockSpec((1, d), const),
            pl.BlockSpec(w_up.shape, const, pipeline_mode=single),
            pl.BlockSpec(w_down.shape, const, pipeline_mode=single),
            pl.BlockSpec((1, d), const),
        ],
        out_specs=pl.BlockSpec((tm, d), lambda i: (i, 0)),
        out_shape=jax.ShapeDtypeStruct((rows, d), F32),
        compiler_params=pltpu.CompilerParams(
            dimension_semantics=("parallel",),
            vmem_limit_bytes=V7X_VMEM_LIMIT),
        name="mlp_final" if final else "mlp",
    )(h, nw, w_up, w_down, final_nw)


def _rope_tables(seq, tm):
    half = MLA_ROPE // 2
    inv = ROPE_THETA ** (-jnp.arange(0, MLA_ROPE, 2, dtype=F32) / MLA_ROPE)
    tok_pos = jnp.arange(N_META, N_META + seq, dtype=F32)
    meta_pos = jnp.maximum(jnp.arange(CHUNK, dtype=F32) - META_PAD, 0.0)
    pos = jnp.concatenate([tok_pos, jnp.tile(meta_pos, tm // CHUNK)])
    ang = pos[:, None] * inv[None, :]
    cos, sin = jnp.cos(ang), jnp.sin(ang)
    zero = jnp.zeros_like(cos)
    assert 4 * half == 128
    return (jnp.concatenate([cos, zero, cos, zero], axis=1),
            jnp.concatenate([-sin, zero, sin, zero], axis=1))


def _spread_rope_cols(w):
    half = MLA_ROPE // 2
    zero = jnp.zeros(w.shape[:-1] + (half,), w.dtype)
    return jnp.concatenate([w[..., :half], zero, w[..., half:], zero], axis=-1)


def kernel(x, meta_tokens, norm_mix_w, w_in, conv_w, conv_b, dt_bias, a_log, d_skip, ssm_norm_w,
           q_norm_w, kv_norm_w, w_uq, w_ukv, w_branch_ssm, w_branch_mla, w_out, norm_mlp_w,
           w_mlp_up, w_mlp_down, final_norm_w):
    batch, seq, d = x.shape
    depth = w_in.shape[0]
    d_inner = w_branch_ssm.shape[1]
    n_heads = dt_bias.shape[1]
    q_lora = q_norm_w.shape[1]
    kv_lora = kv_norm_w.shape[1]
    gn = SSM_GROUPS * SSM_STATE
    conv_dim = d_inner + 2 * gn
    n_tok = batch * seq
    n_meta_rows = batch * CHUNK
    assert d_inner == n_heads * SSM_HEAD_DIM and conv_w.shape[2] == conv_dim
    assert n_heads <= 128 and (d_inner // SSM_GROUPS // SSM_HEAD_DIM) % 2 == 0

    tm = min(1024, n_meta_rows)
    tq = 256
    assert seq % tm == 0 and n_meta_rows % tm == 0 and tm % CHUNK == 0 and seq % tq == 0

    meta_chunk = jnp.concatenate([jnp.zeros((META_PAD, d), x.dtype), meta_tokens.astype(x.dtype)], axis=0)
    h = jnp.concatenate([x.reshape(n_tok, d), jnp.tile(meta_chunk, (batch, 1))], axis=0)

    o_z = 0
    o_xbc = o_z + d_inner
    o_dt = o_xbc + conv_dim
    o_cq = o_dt + n_heads
    o_ckv = o_cq + q_lora
    o_kr = o_ckv + kv_lora
    o_gs = o_kr + MLA_ROPE
    o_gm = o_gs + d
    assert o_gm + d == w_in.shape[2]
    cq_col = d_inner + conv_dim
    ckv_col = cq_col + q_lora
    gs_col = -(-(ckv_col + kv_lora) // d) * d
    gm_col = gs_col + d
    n_main = gm_col + d
    tn = 1024
    assert n_main % tn == 0 and cq_col % q_lora == 0 and ckv_col % kv_lora == 0

    tm_rope = min(512, tm)
    cos_tab, sin_tab = _rope_tables(seq, tm_rope)
    head_of_lane = jnp.arange(d_inner) // SSM_HEAD_DIM
    expand1 = (jnp.arange(128)[:, None] == head_of_lane[None, :]).astype(BF16)
    expand = jnp.concatenate([expand1, expand1, expand1], axis=0)

    out = None
    for i in range(depth):
        wi = w_in[i]
        w_main = jnp.concatenate([
            wi[:, o_z:o_dt], wi[:, o_cq:o_kr],
            jnp.zeros((d, gs_col - ckv_col - kv_lora), wi.dtype),
            wi[:, o_gs:]], axis=1).astype(BF16)
        w_small = jnp.concatenate([
            _spread_rope_cols(wi[:, o_kr:o_gs]), wi[:, o_dt:o_cq],
            jnp.zeros((d, 128 - n_heads), wi.dtype)], axis=1).astype(BF16)
        dtb = jnp.pad(dt_bias[i], (0, 128 - n_heads))[None]
        alog = jnp.pad(a_log[i], (0, 128 - n_heads))[None]
        dskip = jnp.repeat(d_skip[i], SSM_HEAD_DIM)[None]
        wq = w_uq[i].reshape(q_lora, MLA_HEADS, MLA_NOPE + MLA_ROPE)
        wq = jnp.concatenate([wq[..., :MLA_NOPE], _spread_rope_cols(wq[..., MLA_NOPE:])], axis=-1)
        wq = wq.reshape(q_lora, MLA_HEADS * QK_SLAB).astype(BF16)
        wkv = w_ukv[i].reshape(kv_lora, MLA_HEADS, MLA_NOPE + MLA_V)
        wkv = jnp.concatenate([wkv[..., :MLA_NOPE].reshape(kv_lora, -1),
                               wkv[..., MLA_NOPE:].reshape(kv_lora, -1)], axis=1).astype(BF16)

        proj, small = _in_proj(h, norm_mix_w[i][None], w_main, w_small, tm, tn)
        y_ssm = _ssd(proj, small, conv_w[i], conv_b[i][None], dtb, alog, dskip, ssm_norm_w[i][None],
                     expand, batch=batch, seq=seq, d_inner=d_inner)
        q, k, v = _mla_proj(proj, small, q_norm_w[i][None], kv_norm_w[i][None], wq, wkv, cos_tab, sin_tab,
                            tm=tm_rope, seq=seq, n_tok=n_tok, cq_col=cq_col, ckv_col=ckv_col,
                            q_lora=q_lora, kv_lora=kv_lora)
        y_tok, y_meta = _attention(q, k, v, batch=batch, seq=seq, n_tok=n_tok, tq=tq)
        h = _mix(y_ssm, y_tok, y_meta, proj, h, w_branch_ssm[i].astype(BF16), w_branch_mla[i].astype(BF16),
                 w_out[i].astype(BF16), tm=tm, n_tok=n_tok, gs_col=gs_col, gm_col=gm_col)
        last = i == depth - 1
        if last:
            out = _mlp(h, norm_mlp_w[i][None], w_mlp_up[i].astype(BF16), w_mlp_down[i].astype(BF16),
                       final_norm_w[None], tm=tm, rows=n_tok, final=True)
        else:
            h = _mlp(h, norm_mlp_w[i][None], w_mlp_up[i].astype(BF16), w_mlp_down[i].astype(BF16),
                     final_norm_w[None], tm=tm, rows=h.shape[0], final=False)
    return out.reshape(batch, seq, d)
```

```python
import functools

import jax
import jax.numpy as jnp
import numpy as np
from jax import lax
from jax.experimental import pallas as pl
from jax.experimental.pallas import tpu as pltpu

F32 = jnp.float32
BF16 = jnp.bfloat16

N_META = 16
EPS = 1e-6
CHUNK = 128
META_PAD = CHUNK - N_META
SSM_HEAD_DIM = 64
SSM_GROUPS = 4
SSM_STATE = 128
SSM_CONV = 4
MLA_HEADS = 8
MLA_NOPE = 128
MLA_ROPE = 64
MLA_V = 128
ROPE_THETA = 10000.0
QK_SLAB = 256
MASK_NEG = -1e30
V7X_VMEM_LIMIT = 56 * 1024 * 1024


def _rms(x, w):
    var = jnp.mean(x * x, axis=-1, keepdims=True)
    return x * lax.rsqrt(var + EPS) * w


def _sigmoid(x):
    return 1.0 / (1.0 + jnp.exp(-x))


def _split3(x):
    hi = x.astype(BF16)
    r1 = x - hi.astype(F32)
    mid = r1.astype(BF16)
    lo = (r1 - mid.astype(F32)).astype(BF16)
    return hi, mid, lo


def _dot(a, b):
    return jnp.dot(a, b, preferred_element_type=F32)


def _dot_nt(a, b):
    return lax.dot_general(a, b, (((1,), (1,)), ((), ())), preferred_element_type=F32)


def _dot_tn(a, b):
    return lax.dot_general(a, b, (((0,), (0,)), ((), ())), preferred_element_type=F32)


def _split_rows(n_tok, n_meta, tm, meta_base_rows):
    n_tok_tiles = n_tok // tm
    n_meta_tiles = n_meta // tm
    base = meta_base_rows // tm
    tok = lambda i: jnp.minimum(i, n_tok_tiles - 1)
    meta = lambda i: base + jnp.clip(i - n_tok_tiles, 0, n_meta_tiles - 1)
    return n_tok_tiles, tok, meta


def _in_proj_body(ht_ref, hm_ref, nw_ref, w_ref, ws_ref, o_ref, os_ref, u_ref, *, n_tok_tiles):
    is_meta = pl.program_id(0) >= n_tok_tiles

    @pl.when(pl.program_id(1) == 0)
    def _():
        h = jnp.where(is_meta, hm_ref[...], ht_ref[...])
        u = _rms(h, nw_ref[...]).astype(BF16)
        u_ref[...] = u
        os_ref[...] = _dot(u, ws_ref[...])

    o_ref[...] = _dot(u_ref[...], w_ref[...]).astype(o_ref.dtype)


def _in_proj(h_tok, h_meta, nw, w_main, w_small, *, tm, tn, n_tok, n_meta, meta_base_rows):
    d = h_tok.shape[1]
    m = n_tok + n_meta
    n = w_main.shape[1]
    ns = w_small.shape[1]
    n_tok_tiles, tok, meta = _split_rows(n_tok, n_meta, tm, meta_base_rows)
    return pl.pallas_call(
        functools.partial(_in_proj_body, n_tok_tiles=n_tok_tiles),
        grid=(m // tm, n // tn),
        in_specs=[
            pl.BlockSpec((tm, d), lambda i, j: (tok(i), 0)),
            pl.BlockSpec((tm, d), lambda i, j: (meta(i), 0)),
            pl.BlockSpec((1, d), lambda i, j: (0, 0)),
            pl.BlockSpec((d, tn), lambda i, j: (0, j)),
            pl.BlockSpec((d, ns), lambda i, j: (0, 0)),
        ],
        out_specs=[
            pl.BlockSpec((tm, tn), lambda i, j: (i, j)),
            pl.BlockSpec((tm, ns), lambda i, j: (i, 0)),
        ],
        out_shape=[
            jax.ShapeDtypeStruct((m, n), BF16),
            jax.ShapeDtypeStruct((m, ns), F32),
        ],
        scratch_shapes=[pltpu.VMEM((tm, d), BF16)],
        compiler_params=pltpu.CompilerParams(
            dimension_semantics=("parallel", "arbitrary"),
            vmem_limit_bytes=V7X_VMEM_LIMIT),
        name="in_proj",
    )(h_tok, h_meta, nw, w_main, w_small)


def _conv_silu(x, tail8, w, b):
    row8 = lax.broadcasted_iota(jnp.int32, (8, 1), 0)
    x_top = x[0:8]
    acc = b + w[SSM_CONV - 1:SSM_CONV] * x
    acc_top = b + w[SSM_CONV - 1:SSM_CONV] * x_top
    for s in range(1, SSM_CONV):
        wk = w[SSM_CONV - 1 - s:SSM_CONV - s]
        acc = acc + wk * pltpu.roll(x, s, 0)
        top = jnp.where(row8 < s, pltpu.roll(tail8, s, 0), pltpu.roll(x_top, s, 0))
        acc_top = acc_top + wk * top
    y = jnp.concatenate([acc_top, acc[8:]], axis=0)
    return y * _sigmoid(y)


def _ssd_body(z_ref, xs_ref, b_ref, c_ref, dt_ref, cw_ref, cb_ref, dtb_ref, alog_ref,
              dskip_ref, nw_ref, expand_ref, y_ref, state_ref, tail_ref):
    c = pl.program_id(1)
    d_inner = xs_ref.shape[1]
    gn = b_ref.shape[1]
    gw = d_inner // SSM_GROUPS
    heads_per_group = gw // SSM_HEAD_DIM

    @pl.when(c == 0)
    def _():
        state_ref[...] = jnp.zeros_like(state_ref)
        tail_ref[...] = jnp.zeros_like(tail_ref)

    x_raw = xs_ref[...].astype(F32)
    b_raw = b_ref[...].astype(F32)
    c_raw = c_ref[...].astype(F32)
    xs = _conv_silu(x_raw, tail_ref[:, 0:d_inner], cw_ref[:, 0:d_inner], cb_ref[:, 0:d_inner])
    bm = _conv_silu(b_raw, tail_ref[:, d_inner:d_inner + gn],
                    cw_ref[:, d_inner:d_inner + gn], cb_ref[:, d_inner:d_inner + gn])
    cm = _conv_silu(c_raw, tail_ref[:, d_inner + gn:], cw_ref[:, d_inner + gn:], cb_ref[:, d_inner + gn:])
    tail_ref[:, 0:d_inner] = x_raw[CHUNK - 8:]
    tail_ref[:, d_inner:d_inner + gn] = b_raw[CHUNK - 8:]
    tail_ref[:, d_inner + gn:] = c_raw[CHUNK - 8:]

    row = lax.broadcasted_iota(jnp.int32, (CHUNK, 1), 0)
    xdt = dt_ref[...] + dtb_ref[...]
    dt = jnp.maximum(xdt, 0.0) + jnp.log(1.0 + jnp.exp(-jnp.abs(xdt)))
    dt = jnp.where((c > 0) | (row >= META_PAD), dt, 0.0)
    adt = dt * (-jnp.exp(alog_ref[...]))

    ri = lax.broadcasted_iota(jnp.int32, (CHUNK, CHUNK), 0)
    ci = lax.broadcasted_iota(jnp.int32, (CHUNK, CHUNK), 1)
    causal = ri >= ci
    tri = causal.astype(BF16)
    hi, mid, lo = _split3(adt)
    a_cs = _dot(tri, hi) + _dot(tri, mid) + _dot(tri, lo)
    a_cs_t = a_cs.T
    dt_t = dt.T
    exp_a = jnp.exp(a_cs)
    w_state = jnp.exp(a_cs[CHUNK - 1:CHUNK] - a_cs) * dt

    parts = [jnp.concatenate([p, q], axis=0) for p, q in zip(_split3(w_state), _split3(exp_a))]
    expanded = _dot(jnp.concatenate(parts, axis=1), expand_ref[...])
    w_exp = expanded[0:CHUNK]
    exp_a_exp = expanded[CHUNK:]

    xs_b = xs.astype(BF16)
    xw = (xs * w_exp).astype(BF16)
    lane = lax.broadcasted_iota(jnp.int32, (1, 2 * SSM_HEAD_DIM), 1)
    lo_half = lane < SSM_HEAD_DIM

    for g in range(SSM_GROUPS):
        gs = slice(g * gw, (g + 1) * gw)
        bm_g = bm[:, g * SSM_STATE:(g + 1) * SSM_STATE].astype(BF16)
        cm_g = cm[:, g * SSM_STATE:(g + 1) * SSM_STATE].astype(BF16)
        cb = _dot_nt(cm_g, bm_g)
        y_pairs = []
        for pj in range(heads_per_group // 2):
            ms = []
            for e in range(2):
                hd = g * heads_per_group + 2 * pj + e
                seg = a_cs[:, hd:hd + 1] - a_cs_t[hd:hd + 1, :]
                decay = jnp.exp(jnp.where(causal, seg, MASK_NEG))
                ms.append((cb * decay * dt_t[hd:hd + 1, :]).astype(BF16))
            lhs = jnp.concatenate(ms, axis=1)
            p0 = g * gw + pj * 2 * SSM_HEAD_DIM
            xpair = xs_b[:, p0:p0 + 2 * SSM_HEAD_DIM]
            zero = jnp.zeros_like(xpair)
            rhs = jnp.concatenate([jnp.where(lo_half, xpair, zero),
                                   jnp.where(lo_half, zero, xpair)], axis=0)
            y_pairs.append(_dot(lhs, rhs))
        y_diag = jnp.concatenate(y_pairs, axis=1)
        prev = state_ref[:, gs]
        y_off = _dot(cm_g, prev.astype(BF16)) * exp_a_exp[:, gs]
        state_ref[:, gs] = prev * exp_a_exp[CHUNK - 1:CHUNK, gs] + _dot_tn(bm_g, xw[:, gs])
        y = y_diag + y_off + xs[:, gs] * dskip_ref[:, gs]
        zg = z_ref[:, gs].astype(F32)
        y = y * (zg * _sigmoid(zg))
        y_ref[:, gs] = _rms(y, nw_ref[:, gs]).astype(y_ref.dtype)


def _ssd(proj, small, conv_w, conv_b, dt_bias, a_log, d_skip, norm_w, expand, *, batch, seq, d_inner):
    m = proj.shape[0]
    n_tok_chunks = seq // CHUNK
    nc = n_tok_chunks + 1
    gn = SSM_GROUPS * SSM_STATE
    conv_dim = d_inner + 2 * gn
    tok_chunks_total = batch * n_tok_chunks

    def rowblk(b, c):
        return jnp.where(c == 0, tok_chunks_total + b, b * n_tok_chunks + c - 1)

    def col(width_blk):
        return lambda b, c: (rowblk(b, c), width_blk)

    const = lambda b, c: (0, 0)
    return pl.pallas_call(
        _ssd_body,
        grid=(batch, nc),
        in_specs=[
            pl.BlockSpec((CHUNK, d_inner), col(0)),
            pl.BlockSpec((CHUNK, d_inner), col(1)),
            pl.BlockSpec((CHUNK, gn), col(2 * d_inner // gn)),
            pl.BlockSpec((CHUNK, gn), col(2 * d_inner // gn + 1)),
            pl.BlockSpec((CHUNK, 128), col(1)),
            pl.BlockSpec((SSM_CONV, conv_dim), const),
            pl.BlockSpec((1, conv_dim), const),
            pl.BlockSpec((1, 128), const),
            pl.BlockSpec((1, 128), const),
            pl.BlockSpec((1, d_inner), const),
            pl.BlockSpec((1, d_inner), const),
            pl.BlockSpec(expand.shape, const),
        ],
        out_specs=pl.BlockSpec((CHUNK, d_inner), col(0)),
        out_shape=jax.ShapeDtypeStruct((m, d_inner), BF16),
        scratch_shapes=[pltpu.VMEM((SSM_STATE, d_inner), F32),
                        pltpu.VMEM((8, conv_dim), F32)],
        compiler_params=pltpu.CompilerParams(
            dimension_semantics=("parallel", "arbitrary"),
            vmem_limit_bytes=V7X_VMEM_LIMIT),
        name="ssd",
    )(proj, proj, proj, proj, small, conv_w, conv_b, dt_bias, a_log, d_skip, norm_w, expand)


def _rope(x, cos_t, sin_t):
    return x * cos_t + pltpu.roll(x, 64, 1) * sin_t


def _mla_proj_body(cq_ref, ckv_ref, kr_ref, qnw_ref, kvnw_ref, wq_ref, wk_ref, wvt_ref, cos_ref, sin_ref,
                   q_ref, k_ref, vt_ref, *, q_scale):
    cos_t = cos_ref[...]
    sin_t = sin_ref[...]
    cqn = _rms(cq_ref[...].astype(F32), qnw_ref[...]).astype(BF16)
    qf = _dot(cqn, wq_ref[...])
    ckvn = _rms(ckv_ref[...].astype(F32), kvnw_ref[...]).astype(BF16)
    kf = _dot(ckvn, wk_ref[...])
    vt = _dot_nt(wvt_ref[...], ckvn)
    k_pe = _rope(kr_ref[...], cos_t, sin_t).astype(BF16)
    for hd in range(MLA_HEADS):
        o = hd * QK_SLAB
        q_ref[hd, :, 0:MLA_NOPE] = (qf[:, o:o + MLA_NOPE] * q_scale).astype(BF16)
        q_ref[hd, :, MLA_NOPE:QK_SLAB] = (
            _rope(qf[:, o + MLA_NOPE:o + QK_SLAB], cos_t, sin_t) * q_scale).astype(BF16)
        k_ref[hd, :, 0:MLA_NOPE] = kf[:, hd * MLA_NOPE:(hd + 1) * MLA_NOPE].astype(BF16)
        k_ref[hd, :, MLA_NOPE:QK_SLAB] = k_pe
        vt_ref[hd] = vt[hd * MLA_V:(hd + 1) * MLA_V, :].astype(BF16)


def _mla_proj(proj, small, q_norm_w, kv_norm_w, wq, wk, wvt, cos_tab, sin_tab, *, tm, seq, n_tok,
              cq_col, ckv_col, q_lora, kv_lora):
    m = proj.shape[0]
    n_tok_tiles = n_tok // tm
    tiles_per_seq = seq // tm

    def tab(i):
        return (jnp.where(i < n_tok_tiles, i % tiles_per_seq, tiles_per_seq), 0)

    const = lambda i: (0, 0)
    q_scale = float((MLA_NOPE + MLA_ROPE) ** -0.5 * np.log2(np.e))
    return pl.pallas_call(
        functools.partial(_mla_proj_body, q_scale=q_scale),
        grid=(m // tm,),
        in_specs=[
            pl.BlockSpec((tm, q_lora), lambda i: (i, cq_col // q_lora)),
            pl.BlockSpec((tm, kv_lora), lambda i: (i, ckv_col // kv_lora)),
            pl.BlockSpec((tm, 128), lambda i: (i, 0)),
            pl.BlockSpec((1, q_lora), const),
            pl.BlockSpec((1, kv_lora), const),
            pl.BlockSpec(wq.shape, const),
            pl.BlockSpec(wk.shape, const),
            pl.BlockSpec(wvt.shape, const),
            pl.BlockSpec((tm, 128), tab),
            pl.BlockSpec((tm, 128), tab),
        ],
        out_specs=[
            pl.BlockSpec((MLA_HEADS, tm, QK_SLAB), lambda i: (0, i, 0)),
            pl.BlockSpec((MLA_HEADS, tm, QK_SLAB), lambda i: (0, i, 0)),
            pl.BlockSpec((MLA_HEADS, MLA_V, tm), lambda i: (0, 0, i)),
        ],
        out_shape=[
            jax.ShapeDtypeStruct((MLA_HEADS, m, QK_SLAB), BF16),
            jax.ShapeDtypeStruct((MLA_HEADS, m, QK_SLAB), BF16),
            jax.ShapeDtypeStruct((MLA_HEADS, MLA_V, m), BF16),
        ],
        compiler_params=pltpu.CompilerParams(
            dimension_semantics=("parallel",),
            vmem_limit_bytes=V7X_VMEM_LIMIT),
        name="mla_proj",
    )(proj, proj, small, q_norm_w, kv_norm_w, wq, wk, wvt, cos_tab, sin_tab)


ONES_ROWS = 16


def _softmax_t(s_t, n_keys, masks):
    pieces = []
    pos = 0
    for r0, r1, keep in masks:
        if r0 > pos:
            pieces.append(s_t[pos:r0])
        pieces.append(jnp.where(keep, s_t[r0:r1], MASK_NEG))
        pos = r1
    if pos < n_keys:
        pieces.append(s_t[pos:n_keys])
    m = functools.reduce(jnp.maximum, [jnp.max(p, axis=0, keepdims=True) for p in pieces])
    return jnp.concatenate([jnp.exp2(p - m).astype(BF16) for p in pieces], axis=0)


def _pv_t(p_t, vx_ref, n_keys):
    o_t = _dot(vx_ref[:, 0:n_keys], p_t)
    o_t = o_t[0:MLA_V] * (1.0 / o_t[MLA_V:MLA_V + 1])
    return o_t.T


def _attn_body(qt_ref, qm_ref, kt_ref, km_ref, vt_ref, vm_ref, ot_ref, om_ref, kk_ref, vx_ref, *, tq):
    seq = qt_ref.shape[0]
    kk_ref[0:CHUNK, :] = km_ref[...]
    kk_ref[CHUNK:, :] = kt_ref[...]
    vx_ref[0:MLA_V, 0:CHUNK] = vm_ref[...]
    vx_ref[0:MLA_V, CHUNK:] = vt_ref[...]
    vx_ref[MLA_V:, :] = jnp.ones((ONES_ROWS, seq + CHUNK), BF16)

    key_ok = lax.broadcasted_iota(jnp.int32, (CHUNK, 1), 0) >= META_PAD
    rm = lax.broadcasted_iota(jnp.int32, (CHUNK, CHUNK), 0)
    cm = lax.broadcasted_iota(jnp.int32, (CHUNK, CHUNK), 1)
    rq = lax.broadcasted_iota(jnp.int32, (tq, tq), 0)
    cq = lax.broadcasted_iota(jnp.int32, (tq, tq), 1)
    causal_t = rq <= cq
    meta_block = (qm_ref, 0, CHUNK, CHUNK, [(0, CHUNK, (rm <= cm) & (rm >= META_PAD))], om_ref)
    tok_blocks = []
    for qs in range(0, seq, tq):
        n_keys = CHUNK + qs + tq
        tok_blocks.append((qt_ref, qs, tq, n_keys,
                           [(0, CHUNK, key_ok), (n_keys - tq, n_keys, causal_t)], ot_ref))
    blocks = tok_blocks[:1] + tok_blocks[:0:-1] + [meta_block]

    def scores(blk):
        q_ref, qs, rows, n_keys, _, _ = blk
        return _dot_nt(kk_ref[0:n_keys, :], q_ref[qs:qs + rows, :])

    def finish(blk, p_t):
        _, qs, rows, n_keys, _, o_ref = blk
        o_ref[qs:qs + rows, :] = _pv_t(p_t, vx_ref, n_keys).astype(o_ref.dtype)

    s_next = scores(blocks[0])
    p_prev = None
    for bi, blk in enumerate(blocks):
        s_cur = s_next
        if bi + 1 < len(blocks):
            s_next = scores(blocks[bi + 1])
        if p_prev is not None:
            finish(blocks[bi - 1], p_prev)
        p_prev = _softmax_t(s_cur, blk[3], blk[4])
    finish(blocks[-1], p_prev)


def _attention(q, k, vt, *, batch, seq, n_tok, tq):
    lp = seq + CHUNK
    meta0 = n_tok // CHUNK
    tok_rows = lambda b, h: (h, b, 0)
    meta_rows = lambda b, h: (h, meta0 + b, 0)
    return pl.pallas_call(
        functools.partial(_attn_body, tq=tq),
        grid=(batch, MLA_HEADS),
        in_specs=[
            pl.BlockSpec((None, seq, QK_SLAB), tok_rows),
            pl.BlockSpec((None, CHUNK, QK_SLAB), meta_rows),
            pl.BlockSpec((None, seq, QK_SLAB), tok_rows),
            pl.BlockSpec((None, CHUNK, QK_SLAB), meta_rows),
            pl.BlockSpec((None, MLA_V, seq), lambda b, h: (h, 0, b)),
            pl.BlockSpec((None, MLA_V, CHUNK), lambda b, h: (h, 0, meta0 + b)),
        ],
        out_specs=[
            pl.BlockSpec((None, seq, MLA_V), tok_rows),
            pl.BlockSpec((None, CHUNK, MLA_V), tok_rows),
        ],
        out_shape=[
            jax.ShapeDtypeStruct((MLA_HEADS, n_tok, MLA_V), BF16),
            jax.ShapeDtypeStruct((MLA_HEADS, batch * CHUNK, MLA_V), BF16),
        ],
        scratch_shapes=[pltpu.VMEM((lp, QK_SLAB), BF16), pltpu.VMEM((MLA_V + ONES_ROWS, lp), BF16)],
        compiler_params=pltpu.CompilerParams(
            dimension_semantics=("parallel", "parallel"),
            vmem_limit_bytes=V7X_VMEM_LIMIT),
        name="mla_attention",
    )(q, q, k, k, vt, vt)


def _mix_body(ys_ref, ymt_ref, ymm_ref, gs_ref, gm_ref, ht_ref, hm_ref, wbs_ref, wbm_ref, wo_ref, o_ref,
              *, n_tok_tiles):
    i = pl.program_id(0)
    tm = o_ref.shape[0]
    is_meta = i >= n_tok_tiles
    ymt = jnp.concatenate([ymt_ref[hd] for hd in range(MLA_HEADS)], axis=1)
    ymm = jnp.concatenate([ymm_ref[hd] for hd in range(MLA_HEADS)], axis=1)
    ym = jnp.where(is_meta, ymm, ymt)
    h = jnp.where(is_meta, hm_ref[...], ht_ref[...])
    a = _dot(ys_ref[...], wbs_ref[...])
    b = _dot(ym, wbm_ref[...])
    mixed = _sigmoid(gs_ref[...].astype(F32)) * a + _sigmoid(gm_ref[...].astype(F32)) * b
    hn = h + _dot(mixed.astype(BF16), wo_ref[...])
    row = lax.broadcasted_iota(jnp.int32, (tm, 1), 0)
    inert = is_meta & ((row & (CHUNK - 1)) < META_PAD)
    o_ref[...] = jnp.where(inert, 0.0, hn)


def _mix(y_ssm, y_mla_tok, y_mla_meta, proj, h_tok, h_meta, w_bs, w_bm, w_o, *, tm, n_tok, n_meta,
         meta_base_rows, gs_col, gm_col):
    d = h_tok.shape[1]
    m = n_tok + n_meta
    n_tok_tiles, tok, meta = _split_rows(n_tok, n_meta, tm, meta_base_rows)
    _, _, meta0 = _split_rows(n_tok, n_meta, tm, 0)
    const = lambda i: (0, 0)
    single = pl.Buffered(1)
    return pl.pallas_call(
        functools.partial(_mix_body, n_tok_tiles=n_tok_tiles),
        grid=(m // tm,),
        in_specs=[
            pl.BlockSpec((tm, y_ssm.shape[1]), lambda i: (i, 0)),
            pl.BlockSpec((MLA_HEADS, tm, MLA_V), lambda i: (0, tok(i), 0)),
            pl.BlockSpec((MLA_HEADS, tm, MLA_V), lambda i: (0, meta0(i), 0)),
            pl.BlockSpec((tm, d), lambda i: (i, gs_col // d)),
            pl.BlockSpec((tm, d), lambda i: (i, gm_col // d)),
            pl.BlockSpec((tm, d), lambda i: (tok(i), 0)),
            pl.BlockSpec((tm, d), lambda i: (meta(i), 0)),
            pl.BlockSpec(w_bs.shape, const, pipeline_mode=single),
            pl.BlockSpec(w_bm.shape, const, pipeline_mode=single),
            pl.BlockSpec(w_o.shape, const, pipeline_mode=single),
        ],
        out_specs=pl.BlockSpec((tm, d), lambda i: (i, 0)),
        out_shape=jax.ShapeDtypeStruct((m, d), F32),
        compiler_params=pltpu.CompilerParams(
            dimension_semantics=("parallel",),
            vmem_limit_bytes=V7X_VMEM_LIMIT),
        name="mix_out",
    )(y_ssm, y_mla_tok, y_mla_meta, proj, proj, h_tok, h_meta, w_bs, w_bm, w_o)


def _mlp_body(h_ref, nw_ref, wu_ref, wd_ref, fnw_ref, o_ref, *, ff_chunk, final):
    h = h_ref[...]
    v = _rms(h, nw_ref[...]).astype(BF16)
    d_ff = wu_ref.shape[1]
    acc = jnp.zeros_like(h)
    for f0 in range(0, d_ff, ff_chunk):
        a = _dot(v, wu_ref[:, f0:f0 + ff_chunk])
        a = jnp.square(jnp.maximum(a, 0.0)).astype(BF16)
        acc = acc + _dot(a, wd_ref[f0:f0 + ff_chunk, :])
    hn = h + acc
    if final:
        hn = _rms(hn, fnw_ref[...])
    o_ref[...] = hn


def _mlp(h, nw, w_up, w_down, final_nw, *, tm, rows, final):
    d = h.shape[1]
    const = lambda i: (0, 0)
    single = pl.Buffered(1)
    return pl.pallas_call(
        functools.partial(_mlp_body, ff_chunk=1024, final=final),
        grid=(rows // tm,),
        in_specs=[
            pl.BlockSpec((tm, d), lambda i: (i, 0)),
            pl.BlockSpec((1, d), const),
            pl.BlockSpec(w_up.shape, const, pipeline_mode=single),
            pl.BlockSpec(w_down.shape, const, pipeline_mode=single),
            pl.BlockSpec((1, d), const),
        ],
        out_specs=pl.BlockSpec((tm, d), lambda i: (i, 0)),
        out_shape=jax.ShapeDtypeStruct((rows, d), F32),
        compiler_params=pltpu.CompilerParams(
            dimension_semantics=("parallel",),
            vmem_limit_bytes=V7X_VMEM_LIMIT),
        name="mlp_final" if final else "mlp",
    )(h, nw, w_up, w_down, final_nw)


def _rope_tables(seq, tm):
    half = MLA_ROPE // 2
    inv = ROPE_THETA ** (-jnp.arange(0, MLA_ROPE, 2, dtype=F32) / MLA_ROPE)
    tok_pos = jnp.arange(N_META, N_META + seq, dtype=F32)
    meta_pos = jnp.maximum(jnp.arange(CHUNK, dtype=F32) - META_PAD, 0.0)
    pos = jnp.concatenate([tok_pos, jnp.tile(meta_pos, tm // CHUNK)])
    ang = pos[:, None] * inv[None, :]
    cos, sin = jnp.cos(ang), jnp.sin(ang)
    zero = jnp.zeros_like(cos)
    assert 4 * half == 128
    return (jnp.concatenate([cos, zero, cos, zero], axis=1),
            jnp.concatenate([-sin, zero, sin, zero], axis=1))


def _spread_rope_cols(w):
    half = MLA_ROPE // 2
    zero = jnp.zeros(w.shape[:-1] + (half,), w.dtype)
    return jnp.concatenate([w[..., :half], zero, w[..., half:], zero], axis=-1)


def kernel(x, meta_tokens, norm_mix_w, w_in, conv_w, conv_b, dt_bias, a_log, d_skip, ssm_norm_w,
           q_norm_w, kv_norm_w, w_uq, w_ukv, w_branch_ssm, w_branch_mla, w_out, norm_mlp_w,
           w_mlp_up, w_mlp_down, final_norm_w):
    batch, seq, d = x.shape
    depth = w_in.shape[0]
    d_inner = w_branch_ssm.shape[1]
    n_heads = dt_bias.shape[1]
    q_lora = q_norm_w.shape[1]
    kv_lora = kv_norm_w.shape[1]
    gn = SSM_GROUPS * SSM_STATE
    conv_dim = d_inner + 2 * gn
    n_tok = batch * seq
    n_meta_rows = batch * CHUNK
    assert d_inner == n_heads * SSM_HEAD_DIM and conv_w.shape[2] == conv_dim
    assert n_heads <= 128 and (d_inner // SSM_GROUPS // SSM_HEAD_DIM) % 2 == 0

    tm = min(1024, n_meta_rows)
    tq = 256
    assert seq % tm == 0 and n_meta_rows % tm == 0 and tm % CHUNK == 0 and seq % tq == 0

    meta_chunk = jnp.concatenate([jnp.zeros((META_PAD, d), x.dtype), meta_tokens.astype(x.dtype)], axis=0)
    h_tok = x.reshape(n_tok, d)
    h_meta = jnp.tile(meta_chunk, (batch, 1))
    meta_base_rows = 0

    o_z = 0
    o_xbc = o_z + d_inner
    o_dt = o_xbc + conv_dim
    o_cq = o_dt + n_heads
    o_ckv = o_cq + q_lora
    o_kr = o_ckv + kv_lora
    o_gs = o_kr + MLA_ROPE
    o_gm = o_gs + d
    assert o_gm + d == w_in.shape[2]
    cq_col = d_inner + conv_dim
    ckv_col = cq_col + q_lora
    gs_col = -(-(ckv_col + kv_lora) // d) * d
    gm_col = gs_col + d
    n_main = gm_col + d
    tn = 2048
    assert n_main % tn == 0 and cq_col % q_lora == 0 and ckv_col % kv_lora == 0

    tm_rope = min(512, tm)
    cos_tab, sin_tab = _rope_tables(seq, tm_rope)
    head_of_lane = jnp.arange(d_inner) // SSM_HEAD_DIM
    expand1 = (jnp.arange(128)[:, None] == head_of_lane[None, :]).astype(BF16)
    expand = jnp.concatenate([expand1, expand1, expand1], axis=0)

    out = None
    for i in range(depth):
        wi = w_in[i]
        w_main = jnp.concatenate([
            wi[:, o_z:o_dt], wi[:, o_cq:o_kr],
            jnp.zeros((d, gs_col - ckv_col - kv_lora), wi.dtype),
            wi[:, o_gs:]], axis=1).astype(BF16)
        w_small = jnp.concatenate([
            _spread_rope_cols(wi[:, o_kr:o_gs]), wi[:, o_dt:o_cq],
            jnp.zeros((d, 128 - n_heads), wi.dtype)], axis=1).astype(BF16)
        dtb = jnp.pad(dt_bias[i], (0, 128 - n_heads))[None]
        alog = jnp.pad(a_log[i], (0, 128 - n_heads))[None]
        dskip = jnp.repeat(d_skip[i], SSM_HEAD_DIM)[None]
        wq = w_uq[i].reshape(q_lora, MLA_HEADS, MLA_NOPE + MLA_ROPE)
        wq = jnp.concatenate([wq[..., :MLA_NOPE], _spread_rope_cols(wq[..., MLA_NOPE:])], axis=-1)
        wq = wq.reshape(q_lora, MLA_HEADS * QK_SLAB).astype(BF16)
        wkv = w_ukv[i].reshape(kv_lora, MLA_HEADS, MLA_NOPE + MLA_V)
        wk = wkv[..., :MLA_NOPE].reshape(kv_lora, MLA_HEADS * MLA_NOPE).astype(BF16)
        wvt = wkv[..., MLA_NOPE:].reshape(kv_lora, MLA_HEADS * MLA_V).T.astype(BF16)

        proj, small = _in_proj(h_tok, h_meta, norm_mix_w[i][None], w_main, w_small, tm=tm, tn=tn,
                               n_tok=n_tok, n_meta=n_meta_rows, meta_base_rows=meta_base_rows)
        y_ssm = _ssd(proj, small, conv_w[i], conv_b[i][None], dtb, alog, dskip, ssm_norm_w[i][None],
                     expand, batch=batch, seq=seq, d_inner=d_inner)
        q, k, vt = _mla_proj(proj, small, q_norm_w[i][None], kv_norm_w[i][None], wq, wk, wvt, cos_tab, sin_tab,
                             tm=tm_rope, seq=seq, n_tok=n_tok, cq_col=cq_col, ckv_col=ckv_col,
                             q_lora=q_lora, kv_lora=kv_lora)
        y_tok, y_meta = _attention(q, k, vt, batch=batch, seq=seq, n_tok=n_tok, tq=tq)
        h = _mix(y_ssm, y_tok, y_meta, proj, h_tok, h_meta, w_branch_ssm[i].astype(BF16),
                 w_branch_mla[i].astype(BF16), w_out[i].astype(BF16), tm=tm_rope, n_tok=n_tok, n_meta=n_meta_rows,
                 meta_base_rows=meta_base_rows, gs_col=gs_col, gm_col=gm_col)
        last = i == depth - 1
        if last:
            out = _mlp(h, norm_mlp_w[i][None], w_mlp_up[i].astype(BF16), w_mlp_down[i].astype(BF16),
                       final_norm_w[None], tm=tm, rows=n_tok, final=True)
        else:
            h = _mlp(h, norm_mlp_w[i][None], w_mlp_up[i].astype(BF16), w_mlp_down[i].astype(BF16),
                     final_norm_w[None], tm=tm, rows=h.shape[0], final=False)
            h_tok, h_meta, meta_base_rows = h, h, n_tok
    return out.reshape(batch, seq, d)
```

```python
import functools

import jax
import jax.numpy as jnp
import numpy as np
from jax import lax
from jax.experimental import pallas as pl
from jax.experimental.pallas import tpu as pltpu

F32 = jnp.float32
BF16 = jnp.bfloat16

N_META = 16
EPS = 1e-6
CHUNK = 128
META_PAD = CHUNK - N_META
SSM_HEAD_DIM = 64
SSM_GROUPS = 4
SSM_STATE = 128
SSM_CONV = 4
MLA_HEADS = 8
MLA_NOPE = 128
MLA_ROPE = 64
MLA_V = 128
ROPE_THETA = 10000.0
QK_SLAB = 256
MASK_NEG = -1e30
V7X_VMEM_LIMIT = 56 * 1024 * 1024
LOG2E = float(np.log2(np.e))


def _rms(x, w):
    var = jnp.mean(x * x, axis=-1, keepdims=True)
    return x * lax.rsqrt(var + EPS) * w


def _sigmoid(x):
    return 1.0 / (1.0 + jnp.exp(-x))


def _split3(x):
    hi = x.astype(BF16)
    r1 = x - hi.astype(F32)
    mid = r1.astype(BF16)
    lo = (r1 - mid.astype(F32)).astype(BF16)
    return hi, mid, lo


def _dot(a, b):
    return jnp.dot(a, b, preferred_element_type=F32)


def _dot_nt(a, b):
    return lax.dot_general(a, b, (((1,), (1,)), ((), ())), preferred_element_type=F32)


def _dot_tn(a, b):
    return lax.dot_general(a, b, (((0,), (0,)), ((), ())), preferred_element_type=F32)


def _split_rows(n_tok, n_meta, tm, meta_base_rows):
    n_tok_tiles = n_tok // tm
    n_meta_tiles = n_meta // tm
    base = meta_base_rows // tm
    tok = lambda i: jnp.minimum(i, n_tok_tiles - 1)
    meta = lambda i: base + jnp.clip(i - n_tok_tiles, 0, n_meta_tiles - 1)
    return n_tok_tiles, tok, meta


def _in_proj_body(ht_ref, hm_ref, nw_ref, w_ref, ws_ref, o_ref, os_ref, *, n_tok_tiles, silu_cols, tn):
    is_meta = pl.program_id(0) >= n_tok_tiles
    h = jnp.where(is_meta, hm_ref[...], ht_ref[...])
    u = _rms(h, nw_ref[...]).astype(BF16)
    os_ref[...] = _dot(u, ws_ref[...])
    for c0 in range(0, w_ref.shape[1], tn):
        acc = _dot(u, w_ref[:, c0:c0 + tn])
        if c0 < silu_cols:
            acc = acc / (1.0 + jnp.exp2(acc * (-LOG2E)))
        o_ref[:, c0:c0 + tn] = acc.astype(o_ref.dtype)


def _in_proj(h_tok, h_meta, nw, w_main, w_small, *, tm, tn, n_tok, n_meta, meta_base_rows, silu_cols):
    d = h_tok.shape[1]
    m = n_tok + n_meta
    n = w_main.shape[1]
    ns = w_small.shape[1]
    n_tok_tiles, tok, meta = _split_rows(n_tok, n_meta, tm, meta_base_rows)
    single = pl.Buffered(1)
    return pl.pallas_call(
        functools.partial(_in_proj_body, n_tok_tiles=n_tok_tiles, silu_cols=silu_cols, tn=tn),
        grid=(m // tm,),
        in_specs=[
            pl.BlockSpec((tm, d), lambda i: (tok(i), 0)),
            pl.BlockSpec((tm, d), lambda i: (meta(i), 0)),
            pl.BlockSpec((1, d), lambda i: (0, 0)),
            pl.BlockSpec((d, n), lambda i: (0, 0), pipeline_mode=single),
            pl.BlockSpec((d, ns), lambda i: (0, 0), pipeline_mode=single),
        ],
        out_specs=[
            pl.BlockSpec((tm, n), lambda i: (i, 0)),
            pl.BlockSpec((tm, ns), lambda i: (i, 0)),
        ],
        out_shape=[
            jax.ShapeDtypeStruct((m, n), BF16),
            jax.ShapeDtypeStruct((m, ns), F32),
        ],
        compiler_params=pltpu.CompilerParams(
            dimension_semantics=("parallel",),
            vmem_limit_bytes=V7X_VMEM_LIMIT),
        name="in_proj",
    )(h_tok, h_meta, nw, w_main, w_small)


TAIL = 16
HEAD_REPL = 3


def _conv_silu(raw, tail_ref, c0, c1, shift, w, b):
    x_ext = jnp.concatenate([tail_ref[:, c0:c1], raw], axis=0)
    tail_ref[:, c0:c1] = raw[CHUNK - TAIL:, :]
    delayed = _dot(shift, x_ext)
    acc = b[:, c0:c1] + w[SSM_CONV - 1:SSM_CONV, c0:c1] * raw.astype(F32)
    for s in range(1, SSM_CONV):
        acc = acc + w[SSM_CONV - 1 - s:SSM_CONV - s, c0:c1] * delayed[(s - 1) * CHUNK:s * CHUNK]
    return acc / (1.0 + jnp.exp2(acc * (-LOG2E)))


def _ssd_chunk(sz, x_raw, b_raw, c_raw, dt_raw, shift_ref, cw_ref, cb_ref, dtb_ref, alog_ref, dskip_ref,
               nw_ref, expand_ref, state_ref, tail_ref, y_store, *, meta, n_heads):
    d_inner = x_raw.shape[1]
    gn = b_raw.shape[1]
    gw = d_inner // SSM_GROUPS
    heads_per_group = gw // SSM_HEAD_DIM

    shift = shift_ref[...]
    cw = cw_ref[...]
    cb = cb_ref[...]
    xs = _conv_silu(x_raw, tail_ref, 0, d_inner, shift, cw, cb)
    bm = _conv_silu(b_raw, tail_ref, d_inner, d_inner + gn, shift, cw, cb).astype(BF16)
    cm = _conv_silu(c_raw, tail_ref, d_inner + gn, d_inner + 2 * gn, shift, cw, cb).astype(BF16)

    xdt = dt_raw + dtb_ref[...]
    dt = jnp.maximum(xdt, 0.0) + jnp.log(1.0 + jnp.exp(-jnp.abs(xdt)))
    if meta:
        row = lax.broadcasted_iota(jnp.int32, (CHUNK, 1), 0)
        dt = jnp.where(row >= META_PAD, dt, 0.0)
    adt2 = dt * (-LOG2E * jnp.exp(alog_ref[...]))

    ri = lax.broadcasted_iota(jnp.int32, (CHUNK, CHUNK), 0)
    ci = lax.broadcasted_iota(jnp.int32, (CHUNK, CHUNK), 1)
    causal = ri >= ci
    tri = causal.astype(BF16)
    hi, mid, lo = _split3(adt2)
    a_cs = _dot(tri, hi) + _dot(tri, mid) + _dot(tri, lo)
    row_t = (jnp.log2(dt) - a_cs).T
    exp_a = jnp.exp2(a_cs)
    w_state = jnp.exp2(a_cs[CHUNK - 1:CHUNK] - a_cs) * dt

    grp = lax.broadcasted_iota(jnp.int32, (1, 128), 1) // n_heads
    both = jnp.concatenate([w_state, exp_a], axis=0)
    p0, p1, p2 = _split3(both)
    lhs = jnp.where(grp == 0, p0, jnp.where(grp == 1, p1, p2))
    expanded = _dot(lhs, expand_ref[...])
    w_exp = expanded[0:CHUNK]
    exp_a_exp = expanded[CHUNK:]

    xs_b = xs.astype(BF16)
    xw = (xs * w_exp).astype(BF16)
    lane = lax.broadcasted_iota(jnp.int32, (1, 2 * SSM_HEAD_DIM), 1)
    lo_half = lane < SSM_HEAD_DIM

    def intra(g):
        bm_g = bm[:, g * SSM_STATE:(g + 1) * SSM_STATE]
        cm_g = cm[:, g * SSM_STATE:(g + 1) * SSM_STATE]
        cbm = _dot_nt(cm_g, bm_g).astype(BF16)
        y_pairs = []
        for pj in range(heads_per_group // 2):
            ms = []
            for e in range(2):
                hd = g * heads_per_group + 2 * pj + e
                seg = a_cs[:, hd:hd + 1] + row_t[hd:hd + 1, :]
                decay = jnp.exp2(jnp.where(causal, seg, MASK_NEG))
                ms.append(cbm * decay.astype(BF16))
            lhs_p = jnp.concatenate(ms, axis=1)
            p_off = g * gw + pj * 2 * SSM_HEAD_DIM
            xpair = xs_b[:, p_off:p_off + 2 * SSM_HEAD_DIM]
            zero = jnp.zeros_like(xpair)
            rhs = jnp.concatenate([jnp.where(lo_half, xpair, zero),
                                   jnp.where(lo_half, zero, xpair)], axis=0)
            y_pairs.append(_dot(lhs_p, rhs))
        return jnp.concatenate(y_pairs, axis=1)

    def finish(g, y_diag):
        gs = slice(g * gw, (g + 1) * gw)
        bm_g = bm[:, g * SSM_STATE:(g + 1) * SSM_STATE]
        cm_g = cm[:, g * SSM_STATE:(g + 1) * SSM_STATE]
        prev = state_ref[:, gs]
        y_off = _dot(cm_g, prev.astype(BF16)) * exp_a_exp[:, gs]
        state_ref[:, gs] = prev * exp_a_exp[CHUNK - 1:CHUNK, gs] + _dot_tn(bm_g, xw[:, gs])
        y = (y_diag + y_off + xs[:, gs] * dskip_ref[:, gs]) * sz[:, gs].astype(F32)
        y_store(gs, _rms(y, nw_ref[:, gs]).astype(BF16))

    y_next = intra(0)
    for g in range(SSM_GROUPS):
        y_cur = y_next
        if g + 1 < SSM_GROUPS:
            y_next = intra(g + 1)
        finish(g, y_cur)


def _ssd_body(szt_ref, xt_ref, bt_ref, ct_ref, dtt_ref, szm_ref, xm_ref, bm_ref, cm_ref, dtm_ref,
              shift_ref, cw_ref, cb_ref, dtb_ref, alog_ref, dskip_ref, nw_ref, expand_ref,
              yt_ref, ym_ref, state_ref, tail_ref, *, chunks_per_step, n_heads):
    consts = (shift_ref, cw_ref, cb_ref, dtb_ref, alog_ref, dskip_ref, nw_ref, expand_ref, state_ref, tail_ref)

    @pl.when(pl.program_id(1) == 0)
    def _():
        state_ref[...] = jnp.zeros_like(state_ref)
        tail_ref[...] = jnp.zeros_like(tail_ref)

        def store_meta(cols, val):
            ym_ref[:, cols] = val

        _ssd_chunk(szm_ref[...], xm_ref[...], bm_ref[...], cm_ref[...], dtm_ref[...], *consts,
                   store_meta, meta=True, n_heads=n_heads)

    def body(c, carry):
        rows = pl.ds(pl.multiple_of(c * CHUNK, CHUNK), CHUNK)

        def store_tok(cols, val):
            yt_ref[rows, cols] = val

        _ssd_chunk(szt_ref[rows, :], xt_ref[rows, :], bt_ref[rows, :], ct_ref[rows, :], dtt_ref[rows, :],
                   *consts, store_tok, meta=False, n_heads=n_heads)
        return carry

    lax.fori_loop(0, chunks_per_step, body, 0)


def _ssd(proj, small, shift, conv_w, conv_b, dt_bias, a_log, d_skip, norm_w, expand, *, batch, seq, n_tok, d_inner,
         n_heads, chunks_per_step):
    gn = SSM_GROUPS * SSM_STATE
    conv_dim = d_inner + 2 * gn
    rows = chunks_per_step * CHUNK
    steps = seq // rows
    meta0 = n_tok // CHUNK

    def tok(width_blk):
        return lambda b, s: (b * steps + s, width_blk)

    def meta(width_blk):
        return lambda b, s: (meta0 + b, width_blk)

    const = lambda b, s: (0, 0)
    bc_blk = 2 * d_inner // gn
    return pl.pallas_call(
        functools.partial(_ssd_body, chunks_per_step=chunks_per_step, n_heads=n_heads),
        grid=(batch, steps),
        in_specs=[
            pl.BlockSpec((rows, d_inner), tok(0)),
            pl.BlockSpec((rows, d_inner), tok(1)),
            pl.BlockSpec((rows, gn), tok(bc_blk)),
            pl.BlockSpec((rows, gn), tok(bc_blk + 1)),
            pl.BlockSpec((rows, 128), tok(1)),
            pl.BlockSpec((CHUNK, d_inner), meta(0)),
            pl.BlockSpec((CHUNK, d_inner), meta(1)),
            pl.BlockSpec((CHUNK, gn), meta(bc_blk)),
            pl.BlockSpec((CHUNK, gn), meta(bc_blk + 1)),
            pl.BlockSpec((CHUNK, 128), meta(1)),
            pl.BlockSpec(shift.shape, const),
            pl.BlockSpec((SSM_CONV, conv_dim), const),
            pl.BlockSpec((1, conv_dim), const),
            pl.BlockSpec((1, 128), const),
            pl.BlockSpec((1, 128), const),
            pl.BlockSpec((1, d_inner), const),
            pl.BlockSpec((1, d_inner), const),
            pl.BlockSpec(expand.shape, const),
        ],
        out_specs=[
            pl.BlockSpec((rows, d_inner), tok(0)),
            pl.BlockSpec((CHUNK, d_inner), lambda b, s: (b, 0)),
        ],
        out_shape=[
            jax.ShapeDtypeStruct((n_tok, d_inner), BF16),
            jax.ShapeDtypeStruct((batch * CHUNK, d_inner), BF16),
        ],
        scratch_shapes=[pltpu.VMEM((SSM_STATE, d_inner), F32),
                        pltpu.VMEM((TAIL, conv_dim), BF16)],
        compiler_params=pltpu.CompilerParams(
            dimension_semantics=("parallel", "arbitrary"),
            vmem_limit_bytes=V7X_VMEM_LIMIT),
        name="ssd",
    )(proj, proj, proj, proj, small, proj, proj, proj, proj, small,
      shift, conv_w, conv_b, dt_bias, a_log, d_skip, norm_w, expand)


def _rope(x, cos_t, sin_t):
    return x * cos_t + pltpu.roll(x, 64, 1) * sin_t


def _mla_proj_body(cq_ref, ckv_ref, kr_ref, qnw_ref, kvnw_ref, wq_ref, wk_ref, wvt_ref, cos_ref, sin_ref,
                   q_ref, k_ref, vt_ref, *, q_scale):
    cos_t = cos_ref[...]
    sin_t = sin_ref[...]
    cqn = _rms(cq_ref[...].astype(F32), qnw_ref[...]).astype(BF16)
    qf = _dot(cqn, wq_ref[...])
    ckvn = _rms(ckv_ref[...].astype(F32), kvnw_ref[...]).astype(BF16)
    kf = _dot(ckvn, wk_ref[...])
    vt = _dot_nt(wvt_ref[...], ckvn)
    k_pe = _rope(kr_ref[...], cos_t, sin_t).astype(BF16)
    for hd in range(MLA_HEADS):
        o = hd * QK_SLAB
        q_ref[hd, :, 0:MLA_NOPE] = (qf[:, o:o + MLA_NOPE] * q_scale).astype(BF16)
        q_ref[hd, :, MLA_NOPE:QK_SLAB] = (
            _rope(qf[:, o + MLA_NOPE:o + QK_SLAB], cos_t, sin_t) * q_scale).astype(BF16)
        k_ref[hd, :, 0:MLA_NOPE] = kf[:, hd * MLA_NOPE:(hd + 1) * MLA_NOPE].astype(BF16)
        k_ref[hd, :, MLA_NOPE:QK_SLAB] = k_pe
        vt_ref[hd] = vt[hd * MLA_V:(hd + 1) * MLA_V, :].astype(BF16)


def _mla_proj(proj, small, q_norm_w, kv_norm_w, wq, wk, wvt, cos_tab, sin_tab, *, tm, seq, n_tok,
              cq_col, ckv_col, q_lora, kv_lora):
    m = proj.shape[0]
    n_tok_tiles = n_tok // tm
    tiles_per_seq = seq // tm

    def tab(i):
        return (jnp.where(i < n_tok_tiles, i % tiles_per_seq, tiles_per_seq), 0)

    const = lambda i: (0, 0)
    q_scale = float((MLA_NOPE + MLA_ROPE) ** -0.5 * np.log2(np.e))
    return pl.pallas_call(
        functools.partial(_mla_proj_body, q_scale=q_scale),
        grid=(m // tm,),
        in_specs=[
            pl.BlockSpec((tm, q_lora), lambda i: (i, cq_col // q_lora)),
            pl.BlockSpec((tm, kv_lora), lambda i: (i, ckv_col // kv_lora)),
            pl.BlockSpec((tm, 128), lambda i: (i, 0)),
            pl.BlockSpec((1, q_lora), const),
            pl.BlockSpec((1, kv_lora), const),
            pl.BlockSpec(wq.shape, const),
            pl.BlockSpec(wk.shape, const),
            pl.BlockSpec(wvt.shape, const),
            pl.BlockSpec((tm, 128), tab),
            pl.BlockSpec((tm, 128), tab),
        ],
        out_specs=[
            pl.BlockSpec((MLA_HEADS, tm, QK_SLAB), lambda i: (0, i, 0)),
            pl.BlockSpec((MLA_HEADS, tm, QK_SLAB), lambda i: (0, i, 0)),
            pl.BlockSpec((MLA_HEADS, MLA_V, tm), lambda i: (0, 0, i)),
        ],
        out_shape=[
            jax.ShapeDtypeStruct((MLA_HEADS, m, QK_SLAB), BF16),
            jax.ShapeDtypeStruct((MLA_HEADS, m, QK_SLAB), BF16),
            jax.ShapeDtypeStruct((MLA_HEADS, MLA_V, m), BF16),
        ],
        compiler_params=pltpu.CompilerParams(
            dimension_semantics=("parallel",),
            vmem_limit_bytes=V7X_VMEM_LIMIT),
        name="mla_proj",
    )(proj, proj, small, q_norm_w, kv_norm_w, wq, wk, wvt, cos_tab, sin_tab)


ONES_ROWS = 16


def _softmax_t(s_t, n_keys, masks):
    pieces = []
    pos = 0
    for r0, r1, keep in masks:
        if r0 > pos:
            pieces.append(s_t[pos:r0])
        pieces.append(jnp.where(keep, s_t[r0:r1], MASK_NEG))
        pos = r1
    if pos < n_keys:
        pieces.append(s_t[pos:n_keys])
    m = functools.reduce(jnp.maximum, [jnp.max(p, axis=0, keepdims=True) for p in pieces])
    return jnp.concatenate([jnp.exp2(p - m).astype(BF16) for p in pieces], axis=0)


def _pv_t(p_t, vx_ref, n_keys):
    o_t = _dot(vx_ref[:, 0:n_keys], p_t)
    o_t = o_t[0:MLA_V] * (1.0 / o_t[MLA_V:MLA_V + 1])
    return o_t.T


def _attn_body(qt_ref, qm_ref, kt_ref, km_ref, vt_ref, vm_ref, ot_ref, om_ref, kk_ref, vx_ref, *, tq):
    seq = qt_ref.shape[0]
    kk_ref[0:CHUNK, :] = km_ref[...]
    kk_ref[CHUNK:, :] = kt_ref[...]
    vx_ref[0:MLA_V, 0:CHUNK] = vm_ref[...]
    vx_ref[0:MLA_V, CHUNK:] = vt_ref[...]
    vx_ref[MLA_V:, :] = jnp.ones((ONES_ROWS, seq + CHUNK), BF16)

    key_ok = lax.broadcasted_iota(jnp.int32, (CHUNK, 1), 0) >= META_PAD
    rm = lax.broadcasted_iota(jnp.int32, (CHUNK, CHUNK), 0)
    cm = lax.broadcasted_iota(jnp.int32, (CHUNK, CHUNK), 1)
    rq = lax.broadcasted_iota(jnp.int32, (tq, tq), 0)
    cq = lax.broadcasted_iota(jnp.int32, (tq, tq), 1)
    causal_t = rq <= cq
    meta_block = (qm_ref, 0, CHUNK, CHUNK, [(0, CHUNK, (rm <= cm) & (rm >= META_PAD))], om_ref)
    tok_blocks = []
    for qs in range(0, seq, tq):
        n_keys = CHUNK + qs + tq
        tok_blocks.append((qt_ref, qs, tq, n_keys,
                           [(0, CHUNK, key_ok), (n_keys - tq, n_keys, causal_t)], ot_ref))
    blocks = tok_blocks[:1] + tok_blocks[:0:-1] + [meta_block]

    def scores(blk):
        q_ref, qs, rows, n_keys, _, _ = blk
        return _dot_nt(kk_ref[0:n_keys, :], q_ref[qs:qs + rows, :])

    def finish(blk, p_t):
        _, qs, rows, n_keys, _, o_ref = blk
        o_ref[qs:qs + rows, :] = _pv_t(p_t, vx_ref, n_keys).astype(o_ref.dtype)

    s_next = scores(blocks[0])
    p_prev = None
    for bi, blk in enumerate(blocks):
        s_cur = s_next
        if bi + 1 < len(blocks):
            s_next = scores(blocks[bi + 1])
        if p_prev is not None:
            finish(blocks[bi - 1], p_prev)
        p_prev = _softmax_t(s_cur, blk[3], blk[4])
    finish(blocks[-1], p_prev)


def _attention(q, k, vt, *, batch, seq, n_tok, tq):
    lp = seq + CHUNK
    meta0 = n_tok // CHUNK
    tok_rows = lambda b, h: (h, b, 0)
    meta_rows = lambda b, h: (h, meta0 + b, 0)
    return pl.pallas_call(
        functools.partial(_attn_body, tq=tq),
        grid=(batch, MLA_HEADS),
        in_specs=[
            pl.BlockSpec((None, seq, QK_SLAB), tok_rows),
            pl.BlockSpec((None, CHUNK, QK_SLAB), meta_rows),
            pl.BlockSpec((None, seq, QK_SLAB), tok_rows),
            pl.BlockSpec((None, CHUNK, QK_SLAB), meta_rows),
            pl.BlockSpec((None, MLA_V, seq), lambda b, h: (h, 0, b)),
            pl.BlockSpec((None, MLA_V, CHUNK), lambda b, h: (h, 0, meta0 + b)),
        ],
        out_specs=[
            pl.BlockSpec((None, seq, MLA_V), tok_rows),
            pl.BlockSpec((None, CHUNK, MLA_V), tok_rows),
        ],
        out_shape=[
            jax.ShapeDtypeStruct((MLA_HEADS, n_tok, MLA_V), BF16),
            jax.ShapeDtypeStruct((MLA_HEADS, batch * CHUNK, MLA_V), BF16),
        ],
        scratch_shapes=[pltpu.VMEM((lp, QK_SLAB), BF16), pltpu.VMEM((MLA_V + ONES_ROWS, lp), BF16)],
        compiler_params=pltpu.CompilerParams(
            dimension_semantics=("parallel", "parallel"),
            vmem_limit_bytes=V7X_VMEM_LIMIT),
        name="mla_attention",
    )(q, q, k, k, vt, vt)


def _mix_body(yst_ref, ysm_ref, ymt_ref, ymm_ref, gs_ref, gm_ref, ht_ref, hm_ref, wbs_ref, wbm_ref, wo_ref,
              o_ref, *, n_tok_tiles):
    i = pl.program_id(0)
    tm = o_ref.shape[0]
    is_meta = i >= n_tok_tiles
    ymt = jnp.concatenate([ymt_ref[hd] for hd in range(MLA_HEADS)], axis=1)
    ymm = jnp.concatenate([ymm_ref[hd] for hd in range(MLA_HEADS)], axis=1)
    ym = jnp.where(is_meta, ymm, ymt)
    h = jnp.where(is_meta, hm_ref[...], ht_ref[...])
    ys = jnp.where(is_meta, ysm_ref[...], yst_ref[...])
    a = _dot(ys, wbs_ref[...])
    b = _dot(ym, wbm_ref[...])
    mixed = _sigmoid(gs_ref[...].astype(F32)) * a + _sigmoid(gm_ref[...].astype(F32)) * b
    hn = h + _dot(mixed.astype(BF16), wo_ref[...])
    row = lax.broadcasted_iota(jnp.int32, (tm, 1), 0)
    inert = is_meta & ((row & (CHUNK - 1)) < META_PAD)
    o_ref[...] = jnp.where(inert, 0.0, hn)


def _mix(y_ssm_tok, y_ssm_meta, y_mla_tok, y_mla_meta, proj, h_tok, h_meta, w_bs, w_bm, w_o, *, tm, n_tok, n_meta,
         meta_base_rows, gs_col, gm_col):
    d = h_tok.shape[1]
    m = n_tok + n_meta
    n_tok_tiles, tok, meta = _split_rows(n_tok, n_meta, tm, meta_base_rows)
    _, _, meta0 = _split_rows(n_tok, n_meta, tm, 0)
    const = lambda i: (0, 0)
    single = pl.Buffered(1)
    return pl.pallas_call(
        functools.partial(_mix_body, n_tok_tiles=n_tok_tiles),
        grid=(m // tm,),
        in_specs=[
            pl.BlockSpec((tm, y_ssm_tok.shape[1]), lambda i: (tok(i), 0)),
            pl.BlockSpec((tm, y_ssm_tok.shape[1]), lambda i: (meta0(i), 0)),
            pl.BlockSpec((MLA_HEADS, tm, MLA_V), lambda i: (0, tok(i), 0)),
            pl.BlockSpec((MLA_HEADS, tm, MLA_V), lambda i: (0, meta0(i), 0)),
            pl.BlockSpec((tm, d), lambda i: (i, gs_col // d)),
            pl.BlockSpec((tm, d), lambda i: (i, gm_col // d)),
            pl.BlockSpec((tm, d), lambda i: (tok(i), 0)),
            pl.BlockSpec((tm, d), lambda i: (meta(i), 0)),
            pl.BlockSpec(w_bs.shape, const, pipeline_mode=single),
            pl.BlockSpec(w_bm.shape, const, pipeline_mode=single),
            pl.BlockSpec(w_o.shape, const, pipeline_mode=single),
        ],
        out_specs=pl.BlockSpec((tm, d), lambda i: (i, 0)),
        out_shape=jax.ShapeDtypeStruct((m, d), F32),
        compiler_params=pltpu.CompilerParams(
            dimension_semantics=("parallel",),
            vmem_limit_bytes=V7X_VMEM_LIMIT),
        name="mix_out",
    )(y_ssm_tok, y_ssm_meta, y_mla_tok, y_mla_meta, proj, proj, h_tok, h_meta, w_bs, w_bm, w_o)


def _mlp_body(h_ref, nw_ref, wu_ref, wd_ref, fnw_ref, o_ref, *, ff_chunk, final):
    h = h_ref[...]
    v = _rms(h, nw_ref[...]).astype(BF16)
    d_ff = wu_ref.shape[1]
    acc = jnp.zeros_like(h)
    for f0 in range(0, d_ff, ff_chunk):
        a = _dot(v, wu_ref[:, f0:f0 + ff_chunk])
        a = jnp.square(jnp.maximum(a, 0.0)).astype(BF16)
        acc = acc + _dot(a, wd_ref[f0:f0 + ff_chunk, :])
    hn = h + acc
    if final:
        hn = _rms(hn, fnw_ref[...])
    o_ref[...] = hn


def _mlp(h, nw, w_up, w_down, final_nw, *, tm, rows, final):
    d = h.shape[1]
    const = lambda i: (0, 0)
    single = pl.Buffered(1)
    return pl.pallas_call(
        functools.partial(_mlp_body, ff_chunk=1024, final=final),
        grid=(rows // tm,),
        in_specs=[
            pl.BlockSpec((tm, d), lambda i: (i, 0)),
            pl.BlockSpec((1, d), const),
            pl.BlockSpec(w_up.shape, const, pipeline_mode=single),
            pl.BlockSpec(w_down.shape, const, pipeline_mode=single),
            pl.BlockSpec((1, d), const),
        ],
        out_specs=pl.BlockSpec((tm, d), lambda i: (i, 0)),
        out_shape=jax.ShapeDtypeStruct((rows, d), F32),
        compiler_params=pltpu.CompilerParams(
            dimension_semantics=("parallel",),
            vmem_limit_bytes=V7X_VMEM_LIMIT),
        name="mlp_final" if final else "mlp",
    )(h, nw, w_up, w_down, final_nw)


def _rope_tables(seq, tm):
    half = MLA_ROPE // 2
    inv = ROPE_THETA ** (-jnp.arange(0, MLA_ROPE, 2, dtype=F32) / MLA_ROPE)
    tok_pos = jnp.arange(N_META, N_META + seq, dtype=F32)
    meta_pos = jnp.maximum(jnp.arange(CHUNK, dtype=F32) - META_PAD, 0.0)
    pos = jnp.concatenate([tok_pos, jnp.tile(meta_pos, tm // CHUNK)])
    ang = pos[:, None] * inv[None, :]
    cos, sin = jnp.cos(ang), jnp.sin(ang)
    zero = jnp.zeros_like(cos)
    assert 4 * half == 128
    return (jnp.concatenate([cos, zero, cos, zero], axis=1),
            jnp.concatenate([-sin, zero, sin, zero], axis=1))


def _spread_rope_cols(w):
    half = MLA_ROPE // 2
    zero = jnp.zeros(w.shape[:-1] + (half,), w.dtype)
    return jnp.concatenate([w[..., :half], zero, w[..., half:], zero], axis=-1)


def kernel(x, meta_tokens, norm_mix_w, w_in, conv_w, conv_b, dt_bias, a_log, d_skip, ssm_norm_w,
           q_norm_w, kv_norm_w, w_uq, w_ukv, w_branch_ssm, w_branch_mla, w_out, norm_mlp_w,
           w_mlp_up, w_mlp_down, final_norm_w):
    batch, seq, d = x.shape
    depth = w_in.shape[0]
    d_inner = w_branch_ssm.shape[1]
    n_heads = dt_bias.shape[1]
    q_lora = q_norm_w.shape[1]
    kv_lora = kv_norm_w.shape[1]
    gn = SSM_GROUPS * SSM_STATE
    conv_dim = d_inner + 2 * gn
    n_tok = batch * seq
    n_meta_rows = batch * CHUNK
    assert d_inner == n_heads * SSM_HEAD_DIM and conv_w.shape[2] == conv_dim
    assert n_heads <= 128 and (d_inner // SSM_GROUPS // SSM_HEAD_DIM) % 2 == 0

    tm = min(1024, n_meta_rows)
    tq = 256
    assert seq % tm == 0 and n_meta_rows % tm == 0 and tm % CHUNK == 0 and seq % tq == 0

    meta_chunk = jnp.concatenate([jnp.zeros((META_PAD, d), x.dtype), meta_tokens.astype(x.dtype)], axis=0)
    h_tok = x.reshape(n_tok, d)
    h_meta = jnp.tile(meta_chunk, (batch, 1))
    meta_base_rows = 0

    o_z = 0
    o_xbc = o_z + d_inner
    o_dt = o_xbc + conv_dim
    o_cq = o_dt + n_heads
    o_ckv = o_cq + q_lora
    o_kr = o_ckv + kv_lora
    o_gs = o_kr + MLA_ROPE
    o_gm = o_gs + d
    assert o_gm + d == w_in.shape[2]
    cq_col = d_inner + conv_dim
    ckv_col = cq_col + q_lora
    gs_col = -(-(ckv_col + kv_lora) // d) * d
    gm_col = gs_col + d
    n_main = gm_col + d
    tn = 1024
    assert n_main % tn == 0 and d_inner % tn == 0 and cq_col % q_lora == 0 and ckv_col % kv_lora == 0

    tm_rope = min(512, tm)
    cos_tab, sin_tab = _rope_tables(seq, tm_rope)
    assert HEAD_REPL * n_heads <= 128
    head_of_lane = jnp.arange(d_inner) // SSM_HEAD_DIM
    src_lane = jnp.arange(128)
    expand = ((src_lane % n_heads)[:, None] == head_of_lane[None, :]) & (src_lane < HEAD_REPL * n_heads)[:, None]
    expand = expand.astype(BF16)
    rr = jnp.arange((SSM_CONV - 1) * CHUNK)
    shift = (jnp.arange(TAIL + CHUNK)[None, :] == (TAIL + rr % CHUNK - (rr // CHUNK + 1))[:, None]).astype(BF16)

    out = None
    for i in range(depth):
        wi = w_in[i]
        w_main = jnp.concatenate([
            wi[:, o_z:o_dt], wi[:, o_cq:o_kr],
            jnp.zeros((d, gs_col - ckv_col - kv_lora), wi.dtype),
            wi[:, o_gs:]], axis=1).astype(BF16)
        w_small = jnp.concatenate([
            _spread_rope_cols(wi[:, o_kr:o_gs])] + [wi[:, o_dt:o_cq]] * HEAD_REPL + [
            jnp.zeros((d, 128 - HEAD_REPL * n_heads), wi.dtype)], axis=1).astype(BF16)
        dtb = jnp.pad(jnp.tile(dt_bias[i], HEAD_REPL), (0, 128 - HEAD_REPL * n_heads))[None]
        alog = jnp.pad(jnp.tile(a_log[i], HEAD_REPL), (0, 128 - HEAD_REPL * n_heads))[None]
        dskip = jnp.repeat(d_skip[i], SSM_HEAD_DIM)[None]
        wq = w_uq[i].reshape(q_lora, MLA_HEADS, MLA_NOPE + MLA_ROPE)
        wq = jnp.concatenate([wq[..., :MLA_NOPE], _spread_rope_cols(wq[..., MLA_NOPE:])], axis=-1)
        wq = wq.reshape(q_lora, MLA_HEADS * QK_SLAB).astype(BF16)
        wkv = w_ukv[i].reshape(kv_lora, MLA_HEADS, MLA_NOPE + MLA_V)
        wk = wkv[..., :MLA_NOPE].reshape(kv_lora, MLA_HEADS * MLA_NOPE).astype(BF16)
        wvt = wkv[..., MLA_NOPE:].reshape(kv_lora, MLA_HEADS * MLA_V).T.astype(BF16)

        proj, small = _in_proj(h_tok, h_meta, norm_mix_w[i][None], w_main, w_small, tm=tm_rope, tn=tn,
                               n_tok=n_tok, n_meta=n_meta_rows, meta_base_rows=meta_base_rows, silu_cols=d_inner)
        ys_tok, ys_meta = _ssd(proj, small, shift, conv_w[i].astype(F32), conv_b[i][None], dtb, alog, dskip,
                               ssm_norm_w[i][None], expand, batch=batch, seq=seq, n_tok=n_tok, d_inner=d_inner,
                               n_heads=n_heads, chunks_per_step=min(8, seq // CHUNK))
        q, k, vt = _mla_proj(proj, small, q_norm_w[i][None], kv_norm_w[i][None], wq, wk, wvt, cos_tab, sin_tab,
                             tm=tm_rope, seq=seq, n_tok=n_tok, cq_col=cq_col, ckv_col=ckv_col,
                             q_lora=q_lora, kv_lora=kv_lora)
        y_tok, y_meta = _attention(q, k, vt, batch=batch, seq=seq, n_tok=n_tok, tq=tq)
        h = _mix(ys_tok, ys_meta, y_tok, y_meta, proj, h_tok, h_meta, w_branch_ssm[i].astype(BF16),
                 w_branch_mla[i].astype(BF16), w_out[i].astype(BF16), tm=tm_rope, n_tok=n_tok, n_meta=n_meta_rows,
                 meta_base_rows=meta_base_rows, gs_col=gs_col, gm_col=gm_col)
        last = i == depth - 1
        if last:
            out = _mlp(h, norm_mlp_w[i][None], w_mlp_up[i].astype(BF16), w_mlp_down[i].astype(BF16),
                       final_norm_w[None], tm=tm, rows=n_tok, final=True)
        else:
            h = _mlp(h, norm_mlp_w[i][None], w_mlp_up[i].astype(BF16), w_mlp_down[i].astype(BF16),
                     final_norm_w[None], tm=tm, rows=h.shape[0], final=False)
            h_tok, h_meta, meta_base_rows = h, h, n_tok
    return out.reshape(batch, seq, d)
```

```python
import functools

import jax
import jax.numpy as jnp
import numpy as np
from jax import lax
from jax.experimental import pallas as pl
from jax.experimental.pallas import tpu as pltpu

F32 = jnp.float32
BF16 = jnp.bfloat16

N_META = 16
EPS = 1e-6
CHUNK = 128
META_PAD = CHUNK - N_META
SSM_HEAD_DIM = 64
SSM_GROUPS = 4
SSM_STATE = 128
SSM_CONV = 4
MLA_HEADS = 8
MLA_NOPE = 128
MLA_ROPE = 64
MLA_V = 128
ROPE_THETA = 10000.0
QK_SLAB = 256
MASK_NEG = -1e30
V7X_VMEM_LIMIT = 56 * 1024 * 1024
LOG2E = float(np.log2(np.e))


def _rms(x, w):
    var = jnp.mean(x * x, axis=-1, keepdims=True)
    return x * lax.rsqrt(var + EPS) * w


def _sigmoid(x):
    return 1.0 / (1.0 + jnp.exp(-x))


def _split3(x):
    hi = x.astype(BF16)
    r1 = x - hi.astype(F32)
    mid = r1.astype(BF16)
    lo = (r1 - mid.astype(F32)).astype(BF16)
    return hi, mid, lo


def _dot(a, b):
    return jnp.dot(a, b, preferred_element_type=F32)


def _dot_nt(a, b):
    return lax.dot_general(a, b, (((1,), (1,)), ((), ())), preferred_element_type=F32)


def _dot_tn(a, b):
    return lax.dot_general(a, b, (((0,), (0,)), ((), ())), preferred_element_type=F32)


def _split_rows(n_tok, n_meta, tm, meta_base_rows):
    n_tok_tiles = n_tok // tm
    n_meta_tiles = n_meta // tm
    base = meta_base_rows // tm
    tok = lambda i: jnp.minimum(i, n_tok_tiles - 1)
    meta = lambda i: base + jnp.clip(i - n_tok_tiles, 0, n_meta_tiles - 1)
    return n_tok_tiles, tok, meta


def _in_proj_body(ht_ref, hm_ref, nw_ref, w_ref, ws_ref, o_ref, os_ref, *, n_tok_tiles, silu_cols, tn):
    is_meta = pl.program_id(0) >= n_tok_tiles
    h = jnp.where(is_meta, hm_ref[...], ht_ref[...])
    u = _rms(h, nw_ref[...]).astype(BF16)
    os_ref[...] = _dot(u, ws_ref[...])
    for c0 in range(0, w_ref.shape[1], tn):
        acc = _dot(u, w_ref[:, c0:c0 + tn])
        if c0 < silu_cols:
            acc = acc / (1.0 + jnp.exp2(acc * (-LOG2E)))
        o_ref[:, c0:c0 + tn] = acc.astype(o_ref.dtype)


def _in_proj(h_tok, h_meta, nw, w_main, w_small, *, tm, tn, n_tok, n_meta, meta_base_rows, silu_cols):
    d = h_tok.shape[1]
    m = n_tok + n_meta
    n = w_main.shape[1]
    ns = w_small.shape[1]
    n_tok_tiles, tok, meta = _split_rows(n_tok, n_meta, tm, meta_base_rows)
    single = pl.Buffered(1)
    return pl.pallas_call(
        functools.partial(_in_proj_body, n_tok_tiles=n_tok_tiles, silu_cols=silu_cols, tn=tn),
        grid=(m // tm,),
        in_specs=[
            pl.BlockSpec((tm, d), lambda i: (tok(i), 0)),
            pl.BlockSpec((tm, d), lambda i: (meta(i), 0)),
            pl.BlockSpec((1, d), lambda i: (0, 0)),
            pl.BlockSpec((d, n), lambda i: (0, 0), pipeline_mode=single),
            pl.BlockSpec((d, ns), lambda i: (0, 0), pipeline_mode=single),
        ],
        out_specs=[
            pl.BlockSpec((tm, n), lambda i: (i, 0)),
            pl.BlockSpec((tm, ns), lambda i: (i, 0)),
        ],
        out_shape=[
            jax.ShapeDtypeStruct((m, n), BF16),
            jax.ShapeDtypeStruct((m, ns), F32),
        ],
        compiler_params=pltpu.CompilerParams(
            dimension_semantics=("parallel",),
            vmem_limit_bytes=V7X_VMEM_LIMIT),
        name="in_proj",
    )(h_tok, h_meta, nw, w_main, w_small)


TAIL = 16
HEAD_REPL = 3


def _conv_silu(raw, tail_ref, c0, c1, shift, w, b):
    x_ext = jnp.concatenate([tail_ref[:, c0:c1], raw], axis=0)
    tail_ref[:, c0:c1] = raw[CHUNK - TAIL:, :]
    delayed = _dot(shift, x_ext)
    acc = b[:, c0:c1] + w[SSM_CONV - 1:SSM_CONV, c0:c1] * raw.astype(F32)
    for s in range(1, SSM_CONV):
        acc = acc + w[SSM_CONV - 1 - s:SSM_CONV - s, c0:c1] * delayed[(s - 1) * CHUNK:s * CHUNK]
    return acc / (1.0 + jnp.exp2(acc * (-LOG2E)))


def _ssd_chunk(sz, x_raw, b_raw, c_raw, dt_raw, shift_ref, cw_ref, cb_ref, dtb_ref, alog_ref, dskip_ref,
               nw_ref, expand_ref, state_ref, tail_ref, y_store, *, meta, n_heads):
    d_inner = x_raw.shape[1]
    gn = b_raw.shape[1]
    gw = d_inner // SSM_GROUPS
    heads_per_group = gw // SSM_HEAD_DIM

    shift = shift_ref[...]
    cw = cw_ref[...]
    cb = cb_ref[...]
    xs = _conv_silu(x_raw, tail_ref, 0, d_inner, shift, cw, cb)
    bm = _conv_silu(b_raw, tail_ref, d_inner, d_inner + gn, shift, cw, cb).astype(BF16)
    cm = _conv_silu(c_raw, tail_ref, d_inner + gn, d_inner + 2 * gn, shift, cw, cb).astype(BF16)

    xdt = dt_raw + dtb_ref[...]
    dt = jnp.maximum(xdt, 0.0) + jnp.log(1.0 + jnp.exp(-jnp.abs(xdt)))
    if meta:
        row = lax.broadcasted_iota(jnp.int32, (CHUNK, 1), 0)
        dt = jnp.where(row >= META_PAD, dt, 0.0)
    adt2 = dt * (-LOG2E * jnp.exp(alog_ref[...]))

    ri = lax.broadcasted_iota(jnp.int32, (CHUNK, CHUNK), 0)
    ci = lax.broadcasted_iota(jnp.int32, (CHUNK, CHUNK), 1)
    causal = ri >= ci
    tri = causal.astype(BF16)
    hi, mid, lo = _split3(adt2)
    a_cs = _dot(tri, hi) + _dot(tri, mid) + _dot(tri, lo)
    row_t = (jnp.log2(dt) - a_cs).T
    exp_a = jnp.exp2(a_cs)
    w_state = jnp.exp2(a_cs[CHUNK - 1:CHUNK] - a_cs) * dt

    grp = lax.broadcasted_iota(jnp.int32, (1, 128), 1) // n_heads
    both = jnp.concatenate([w_state, exp_a], axis=0)
    p0, p1, p2 = _split3(both)
    lhs = jnp.where(grp == 0, p0, jnp.where(grp == 1, p1, p2))
    expanded = _dot(lhs, expand_ref[...])
    w_exp = expanded[0:CHUNK]
    exp_a_exp = expanded[CHUNK:]

    xs_b = xs.astype(BF16)
    xw = (xs * w_exp).astype(BF16)
    lane = lax.broadcasted_iota(jnp.int32, (1, 2 * SSM_HEAD_DIM), 1)
    lo_half = lane < SSM_HEAD_DIM

    def intra(g):
        bm_g = bm[:, g * SSM_STATE:(g + 1) * SSM_STATE]
        cm_g = cm[:, g * SSM_STATE:(g + 1) * SSM_STATE]
        cbm = _dot_nt(cm_g, bm_g).astype(BF16)
        y_pairs = []
        for pj in range(heads_per_group // 2):
            ms = []
            for e in range(2):
                hd = g * heads_per_group + 2 * pj + e
                seg = a_cs[:, hd:hd + 1] + row_t[hd:hd + 1, :]
                decay = jnp.exp2(jnp.where(causal, seg, MASK_NEG))
                ms.append(cbm * decay.astype(BF16))
            lhs_p = jnp.concatenate(ms, axis=1)
            p_off = g * gw + pj * 2 * SSM_HEAD_DIM
            xpair = xs_b[:, p_off:p_off + 2 * SSM_HEAD_DIM]
            zero = jnp.zeros_like(xpair)
            rhs = jnp.concatenate([jnp.where(lo_half, xpair, zero),
                                   jnp.where(lo_half, zero, xpair)], axis=0)
            y_pairs.append(_dot(lhs_p, rhs))
        return jnp.concatenate(y_pairs, axis=1)

    def finish(g, y_diag):
        gs = slice(g * gw, (g + 1) * gw)
        bm_g = bm[:, g * SSM_STATE:(g + 1) * SSM_STATE]
        cm_g = cm[:, g * SSM_STATE:(g + 1) * SSM_STATE]
        prev = state_ref[:, gs]
        y_off = _dot(cm_g, prev.astype(BF16)) * exp_a_exp[:, gs]
        state_ref[:, gs] = prev * exp_a_exp[CHUNK - 1:CHUNK, gs] + _dot_tn(bm_g, xw[:, gs])
        y = (y_diag + y_off + xs[:, gs] * dskip_ref[:, gs]) * sz[:, gs].astype(F32)
        y_store(gs, _rms(y, nw_ref[:, gs]).astype(BF16))

    y_next = intra(0)
    for g in range(SSM_GROUPS):
        y_cur = y_next
        if g + 1 < SSM_GROUPS:
            y_next = intra(g + 1)
        finish(g, y_cur)


def _ssd_body(szt_ref, xt_ref, bt_ref, ct_ref, dtt_ref, szm_ref, xm_ref, bm_ref, cm_ref, dtm_ref,
              shift_ref, cw_ref, cb_ref, dtb_ref, alog_ref, dskip_ref, nw_ref, expand_ref,
              yt_ref, ym_ref, state_ref, tail_ref, *, chunks_per_step, n_heads):
    consts = (shift_ref, cw_ref, cb_ref, dtb_ref, alog_ref, dskip_ref, nw_ref, expand_ref, state_ref, tail_ref)

    @pl.when(pl.program_id(1) == 0)
    def _():
        state_ref[...] = jnp.zeros_like(state_ref)
        tail_ref[...] = jnp.zeros_like(tail_ref)

        def store_meta(cols, val):
            ym_ref[:, cols] = val

        _ssd_chunk(szm_ref[...], xm_ref[...], bm_ref[...], cm_ref[...], dtm_ref[...], *consts,
                   store_meta, meta=True, n_heads=n_heads)

    def body(c, carry):
        rows = pl.ds(pl.multiple_of(c * CHUNK, CHUNK), CHUNK)

        def store_tok(cols, val):
            yt_ref[rows, cols] = val

        _ssd_chunk(szt_ref[rows, :], xt_ref[rows, :], bt_ref[rows, :], ct_ref[rows, :], dtt_ref[rows, :],
                   *consts, store_tok, meta=False, n_heads=n_heads)
        return carry

    lax.fori_loop(0, chunks_per_step, body, 0)


def _ssd(proj, small, shift, conv_w, conv_b, dt_bias, a_log, d_skip, norm_w, expand, *, batch, seq, n_tok, d_inner,
         n_heads, chunks_per_step):
    gn = SSM_GROUPS * SSM_STATE
    conv_dim = d_inner + 2 * gn
    rows = chunks_per_step * CHUNK
    steps = seq // rows
    meta0 = n_tok // CHUNK

    def tok(width_blk):
        return lambda b, s: (b * steps + s, width_blk)

    def meta(width_blk):
        return lambda b, s: (meta0 + b, width_blk)

    const = lambda b, s: (0, 0)
    bc_blk = 2 * d_inner // gn
    return pl.pallas_call(
        functools.partial(_ssd_body, chunks_per_step=chunks_per_step, n_heads=n_heads),
        grid=(batch, steps),
        in_specs=[
            pl.BlockSpec((rows, d_inner), tok(0)),
            pl.BlockSpec((rows, d_inner), tok(1)),
            pl.BlockSpec((rows, gn), tok(bc_blk)),
            pl.BlockSpec((rows, gn), tok(bc_blk + 1)),
            pl.BlockSpec((rows, 128), tok(1)),
            pl.BlockSpec((CHUNK, d_inner), meta(0)),
            pl.BlockSpec((CHUNK, d_inner), meta(1)),
            pl.BlockSpec((CHUNK, gn), meta(bc_blk)),
            pl.BlockSpec((CHUNK, gn), meta(bc_blk + 1)),
            pl.BlockSpec((CHUNK, 128), meta(1)),
            pl.BlockSpec(shift.shape, const),
            pl.BlockSpec((SSM_CONV, conv_dim), const),
            pl.BlockSpec((1, conv_dim), const),
            pl.BlockSpec((1, 128), const),
            pl.BlockSpec((1, 128), const),
            pl.BlockSpec((1, d_inner), const),
            pl.BlockSpec((1, d_inner), const),
            pl.BlockSpec(expand.shape, const),
        ],
        out_specs=[
            pl.BlockSpec((rows, d_inner), tok(0)),
            pl.BlockSpec((CHUNK, d_inner), lambda b, s: (b, 0)),
        ],
        out_shape=[
            jax.ShapeDtypeStruct((n_tok, d_inner), BF16),
            jax.ShapeDtypeStruct((batch * CHUNK, d_inner), BF16),
        ],
        scratch_shapes=[pltpu.VMEM((SSM_STATE, d_inner), F32),
                        pltpu.VMEM((TAIL, conv_dim), BF16)],
        compiler_params=pltpu.CompilerParams(
            dimension_semantics=("parallel", "arbitrary"),
            vmem_limit_bytes=V7X_VMEM_LIMIT),
        name="ssd",
    )(proj, proj, proj, proj, small, proj, proj, proj, proj, small,
      shift, conv_w, conv_b, dt_bias, a_log, d_skip, norm_w, expand)


def _rope(x, cos_t, sin_t):
    return x * cos_t + pltpu.roll(x, 64, 1) * sin_t


def _mla_proj_body(cq_ref, ckv_ref, kr_ref, qnw_ref, kvnw_ref, wq_ref, wk_ref, wvt_ref, cos_ref, sin_ref,
                   q_ref, k_ref, vt_ref, *, q_scale):
    cos_t = cos_ref[...]
    sin_t = sin_ref[...]
    cqn = _rms(cq_ref[...].astype(F32), qnw_ref[...]).astype(BF16)
    qf = _dot(cqn, wq_ref[...])
    ckvn = _rms(ckv_ref[...].astype(F32), kvnw_ref[...]).astype(BF16)
    kf = _dot(ckvn, wk_ref[...])
    vt = _dot_nt(wvt_ref[...], ckvn)
    k_pe = _rope(kr_ref[...], cos_t, sin_t).astype(BF16)
    for hd in range(MLA_HEADS):
        o = hd * QK_SLAB
        q_ref[hd, :, 0:MLA_NOPE] = (qf[:, o:o + MLA_NOPE] * q_scale).astype(BF16)
        q_ref[hd, :, MLA_NOPE:QK_SLAB] = (
            _rope(qf[:, o + MLA_NOPE:o + QK_SLAB], cos_t, sin_t) * q_scale).astype(BF16)
        k_ref[hd, :, 0:MLA_NOPE] = kf[:, hd * MLA_NOPE:(hd + 1) * MLA_NOPE].astype(BF16)
        k_ref[hd, :, MLA_NOPE:QK_SLAB] = k_pe
        vt_ref[hd] = vt[hd * MLA_V:(hd + 1) * MLA_V, :].astype(BF16)


def _mla_proj(proj, small, q_norm_w, kv_norm_w, wq, wk, wvt, cos_tab, sin_tab, *, tm, seq, n_tok,
              cq_col, ckv_col, q_lora, kv_lora):
    m = proj.shape[0]
    n_tok_tiles = n_tok // tm
    tiles_per_seq = seq // tm

    def tab(i):
        return (jnp.where(i < n_tok_tiles, i % tiles_per_seq, tiles_per_seq), 0)

    const = lambda i: (0, 0)
    q_scale = float((MLA_NOPE + MLA_ROPE) ** -0.5 * np.log2(np.e))
    return pl.pallas_call(
        functools.partial(_mla_proj_body, q_scale=q_scale),
        grid=(m // tm,),
        in_specs=[
            pl.BlockSpec((tm, q_lora), lambda i: (i, cq_col // q_lora)),
            pl.BlockSpec((tm, kv_lora), lambda i: (i, ckv_col // kv_lora)),
            pl.BlockSpec((tm, 128), lambda i: (i, 0)),
            pl.BlockSpec((1, q_lora), const),
            pl.BlockSpec((1, kv_lora), const),
            pl.BlockSpec(wq.shape, const),
            pl.BlockSpec(wk.shape, const),
            pl.BlockSpec(wvt.shape, const),
            pl.BlockSpec((tm, 128), tab),
            pl.BlockSpec((tm, 128), tab),
        ],
        out_specs=[
            pl.BlockSpec((MLA_HEADS, tm, QK_SLAB), lambda i: (0, i, 0)),
            pl.BlockSpec((MLA_HEADS, tm, QK_SLAB), lambda i: (0, i, 0)),
            pl.BlockSpec((MLA_HEADS, MLA_V, tm), lambda i: (0, 0, i)),
        ],
        out_shape=[
            jax.ShapeDtypeStruct((MLA_HEADS, m, QK_SLAB), BF16),
            jax.ShapeDtypeStruct((MLA_HEADS, m, QK_SLAB), BF16),
            jax.ShapeDtypeStruct((MLA_HEADS, MLA_V, m), BF16),
        ],
        compiler_params=pltpu.CompilerParams(
            dimension_semantics=("parallel",),
            vmem_limit_bytes=V7X_VMEM_LIMIT),
        name="mla_proj",
    )(proj, proj, small, q_norm_w, kv_norm_w, wq, wk, wvt, cos_tab, sin_tab)


ONES_ROWS = 16


def _softmax_t(s_t, n_keys, masks):
    pieces = []
    pos = 0
    for r0, r1, keep in masks:
        if r0 > pos:
            pieces.append(s_t[pos:r0])
        pieces.append(jnp.where(keep, s_t[r0:r1], MASK_NEG))
        pos = r1
    if pos < n_keys:
        pieces.append(s_t[pos:n_keys])
    m = functools.reduce(jnp.maximum, [jnp.max(p, axis=0, keepdims=True) for p in pieces])
    return jnp.concatenate([jnp.exp2(p - m).astype(BF16) for p in pieces], axis=0)


def _pv_t(p_t, vx_ref, n_keys):
    o_t = _dot(vx_ref[:, 0:n_keys], p_t)
    o_t = o_t[0:MLA_V] * (1.0 / o_t[MLA_V:MLA_V + 1])
    return o_t.T


def _attn_body(qt_ref, qm_ref, kt_ref, km_ref, vt_ref, vm_ref, ot_ref, om_ref, kk_ref, vx_ref, *, tq):
    heads, seq = qt_ref.shape[0], qt_ref.shape[1]
    key_ok = lax.broadcasted_iota(jnp.int32, (CHUNK, 1), 0) >= META_PAD
    rm = lax.broadcasted_iota(jnp.int32, (CHUNK, CHUNK), 0)
    cm = lax.broadcasted_iota(jnp.int32, (CHUNK, CHUNK), 1)
    rq = lax.broadcasted_iota(jnp.int32, (tq, tq), 0)
    cq = lax.broadcasted_iota(jnp.int32, (tq, tq), 1)
    causal_t = rq <= cq

    blocks = []
    for hh in range(heads):
        kk_ref[hh, 0:CHUNK, :] = km_ref[hh]
        kk_ref[hh, CHUNK:, :] = kt_ref[hh]
        vx_ref[hh, 0:MLA_V, 0:CHUNK] = vm_ref[hh]
        vx_ref[hh, 0:MLA_V, CHUNK:] = vt_ref[hh]
        vx_ref[hh, MLA_V:, :] = jnp.ones((ONES_ROWS, seq + CHUNK), BF16)
        meta_block = (hh, qm_ref, 0, CHUNK, CHUNK, [(0, CHUNK, (rm <= cm) & (rm >= META_PAD))], om_ref)
        tok_blocks = []
        for qs in range(0, seq, tq):
            n_keys = CHUNK + qs + tq
            tok_blocks.append((hh, qt_ref, qs, tq, n_keys,
                               [(0, CHUNK, key_ok), (n_keys - tq, n_keys, causal_t)], ot_ref))
        blocks += tok_blocks[:1] + tok_blocks[:0:-1] + [meta_block]

    def scores(blk):
        hh, q_ref, qs, rows, n_keys, _, _ = blk
        return _dot_nt(kk_ref[hh, 0:n_keys, :], q_ref[hh, qs:qs + rows, :])

    def finish(blk, p_t):
        hh, _, qs, rows, n_keys, _, o_ref = blk
        o_ref[hh, qs:qs + rows, :] = _pv_t(p_t, vx_ref.at[hh], n_keys).astype(o_ref.dtype)

    s_next = scores(blocks[0])
    p_prev = None
    for bi, blk in enumerate(blocks):
        s_cur = s_next
        if bi + 1 < len(blocks):
            s_next = scores(blocks[bi + 1])
        if p_prev is not None:
            finish(blocks[bi - 1], p_prev)
        p_prev = _softmax_t(s_cur, blk[4], blk[5])
    finish(blocks[-1], p_prev)


def _attention(q, k, vt, *, batch, seq, n_tok, tq, heads_per_step):
    lp = seq + CHUNK
    meta0 = n_tok // CHUNK
    hps = heads_per_step
    tok_rows = lambda b, h: (h, b, 0)
    meta_rows = lambda b, h: (h, meta0 + b, 0)
    return pl.pallas_call(
        functools.partial(_attn_body, tq=tq),
        grid=(batch, MLA_HEADS // hps),
        in_specs=[
            pl.BlockSpec((hps, seq, QK_SLAB), tok_rows),
            pl.BlockSpec((hps, CHUNK, QK_SLAB), meta_rows),
            pl.BlockSpec((hps, seq, QK_SLAB), tok_rows),
            pl.BlockSpec((hps, CHUNK, QK_SLAB), meta_rows),
            pl.BlockSpec((hps, MLA_V, seq), lambda b, h: (h, 0, b)),
            pl.BlockSpec((hps, MLA_V, CHUNK), lambda b, h: (h, 0, meta0 + b)),
        ],
        out_specs=[
            pl.BlockSpec((hps, seq, MLA_V), tok_rows),
            pl.BlockSpec((hps, CHUNK, MLA_V), tok_rows),
        ],
        out_shape=[
            jax.ShapeDtypeStruct((MLA_HEADS, n_tok, MLA_V), BF16),
            jax.ShapeDtypeStruct((MLA_HEADS, batch * CHUNK, MLA_V), BF16),
        ],
        scratch_shapes=[pltpu.VMEM((hps, lp, QK_SLAB), BF16),
                        pltpu.VMEM((hps, MLA_V + ONES_ROWS, lp), BF16)],
        compiler_params=pltpu.CompilerParams(
            dimension_semantics=("parallel", "parallel"),
            vmem_limit_bytes=V7X_VMEM_LIMIT),
        name="mla_attention",
    )(q, q, k, k, vt, vt)


def _mix_body(yst_ref, ysm_ref, ymt_ref, ymm_ref, gs_ref, gm_ref, ht_ref, hm_ref, wbs_ref, wbm_ref, wo_ref,
              o_ref, *, n_tok_tiles):
    i = pl.program_id(0)
    tm = o_ref.shape[0]
    is_meta = i >= n_tok_tiles
    ymt = jnp.concatenate([ymt_ref[hd] for hd in range(MLA_HEADS)], axis=1)
    ymm = jnp.concatenate([ymm_ref[hd] for hd in range(MLA_HEADS)], axis=1)
    ym = jnp.where(is_meta, ymm, ymt)
    h = jnp.where(is_meta, hm_ref[...], ht_ref[...])
    ys = jnp.where(is_meta, ysm_ref[...], yst_ref[...])
    a = _dot(ys, wbs_ref[...])
    b = _dot(ym, wbm_ref[...])
    mixed = _sigmoid(gs_ref[...].astype(F32)) * a + _sigmoid(gm_ref[...].astype(F32)) * b
    hn = h + _dot(mixed.astype(BF16), wo_ref[...])
    row = lax.broadcasted_iota(jnp.int32, (tm, 1), 0)
    inert = is_meta & ((row & (CHUNK - 1)) < META_PAD)
    o_ref[...] = jnp.where(inert, 0.0, hn)


def _mix(y_ssm_tok, y_ssm_meta, y_mla_tok, y_mla_meta, proj, h_tok, h_meta, w_bs, w_bm, w_o, *, tm, n_tok, n_meta,
         rows, meta_base_rows, gs_col, gm_col):
    d = h_tok.shape[1]
    m = rows
    n_tok_tiles, tok, meta = _split_rows(n_tok, n_meta, tm, meta_base_rows)
    _, _, meta0 = _split_rows(n_tok, n_meta, tm, 0)
    const = lambda i: (0, 0)
    single = pl.Buffered(1)
    return pl.pallas_call(
        functools.partial(_mix_body, n_tok_tiles=n_tok_tiles),
        grid=(m // tm,),
        in_specs=[
            pl.BlockSpec((tm, y_ssm_tok.shape[1]), lambda i: (tok(i), 0)),
            pl.BlockSpec((tm, y_ssm_tok.shape[1]), lambda i: (meta0(i), 0)),
            pl.BlockSpec((MLA_HEADS, tm, MLA_V), lambda i: (0, tok(i), 0)),
            pl.BlockSpec((MLA_HEADS, tm, MLA_V), lambda i: (0, meta0(i), 0)),
            pl.BlockSpec((tm, d), lambda i: (i, gs_col // d)),
            pl.BlockSpec((tm, d), lambda i: (i, gm_col // d)),
            pl.BlockSpec((tm, d), lambda i: (tok(i), 0)),
            pl.BlockSpec((tm, d), lambda i: (meta(i), 0)),
            pl.BlockSpec(w_bs.shape, const, pipeline_mode=single),
            pl.BlockSpec(w_bm.shape, const, pipeline_mode=single),
            pl.BlockSpec(w_o.shape, const, pipeline_mode=single),
        ],
        out_specs=pl.BlockSpec((tm, d), lambda i: (i, 0)),
        out_shape=jax.ShapeDtypeStruct((m, d), F32),
        compiler_params=pltpu.CompilerParams(
            dimension_semantics=("parallel",),
            vmem_limit_bytes=V7X_VMEM_LIMIT),
        name="mix_out",
    )(y_ssm_tok, y_ssm_meta, y_mla_tok, y_mla_meta, proj, proj, h_tok, h_meta, w_bs, w_bm, w_o)


def _mlp_body(h_ref, nw_ref, wu_ref, wd_ref, fnw_ref, o_ref, *, ff_chunk, final):
    h = h_ref[...]
    v = _rms(h, nw_ref[...]).astype(BF16)
    d_ff = wu_ref.shape[1]
    acc = jnp.zeros_like(h)
    for f0 in range(0, d_ff, ff_chunk):
        a = _dot(v, wu_ref[:, f0:f0 + ff_chunk])
        a = jnp.square(jnp.maximum(a, 0.0)).astype(BF16)
        acc = acc + _dot(a, wd_ref[f0:f0 + ff_chunk, :])
    hn = h + acc
    if final:
        hn = _rms(hn, fnw_ref[...])
    o_ref[...] = hn


def _mlp(h, nw, w_up, w_down, final_nw, *, tm, rows, final):
    d = h.shape[1]
    const = lambda i: (0, 0)
    single = pl.Buffered(1)
    return pl.pallas_call(
        functools.partial(_mlp_body, ff_chunk=1024, final=final),
        grid=(rows // tm,),
        in_specs=[
            pl.BlockSpec((tm, d), lambda i: (i, 0)),
            pl.BlockSpec((1, d), const),
            pl.BlockSpec(w_up.shape, const, pipeline_mode=single),
            pl.BlockSpec(w_down.shape, const, pipeline_mode=single),
            pl.BlockSpec((1, d), const),
        ],
        out_specs=pl.BlockSpec((tm, d), lambda i: (i, 0)),
        out_shape=jax.ShapeDtypeStruct((rows, d), F32),
        compiler_params=pltpu.CompilerParams(
            dimension_semantics=("parallel",),
            vmem_limit_bytes=V7X_VMEM_LIMIT),
        name="mlp_final" if final else "mlp",
    )(h, nw, w_up, w_down, final_nw)


def _rope_tables(seq, tm):
    half = MLA_ROPE // 2
    f32 = np.float32
    inv = np.power(f32(ROPE_THETA), -np.arange(0, MLA_ROPE, 2, dtype=f32) / f32(MLA_ROPE)).astype(f32)
    tok_pos = np.arange(N_META, N_META + seq, dtype=f32)
    meta_pos = np.maximum(np.arange(CHUNK, dtype=f32) - f32(META_PAD), f32(0.0))
    pos = np.concatenate([tok_pos, np.tile(meta_pos, tm // CHUNK)])
    ang = (pos[:, None] * inv[None, :]).astype(f32)
    cos, sin = np.cos(ang).astype(f32), np.sin(ang).astype(f32)
    zero = np.zeros_like(cos)
    assert 4 * half == 128
    return (jnp.asarray(np.concatenate([cos, zero, cos, zero], axis=1)),
            jnp.asarray(np.concatenate([-sin, zero, sin, zero], axis=1)))


def _spread_rope_cols(w):
    half = MLA_ROPE // 2
    zero = jnp.zeros(w.shape[:-1] + (half,), w.dtype)
    return jnp.concatenate([w[..., :half], zero, w[..., half:], zero], axis=-1)


def kernel(x, meta_tokens, norm_mix_w, w_in, conv_w, conv_b, dt_bias, a_log, d_skip, ssm_norm_w,
           q_norm_w, kv_norm_w, w_uq, w_ukv, w_branch_ssm, w_branch_mla, w_out, norm_mlp_w,
           w_mlp_up, w_mlp_down, final_norm_w):
    batch, seq, d = x.shape
    depth = w_in.shape[0]
    d_inner = w_branch_ssm.shape[1]
    n_heads = dt_bias.shape[1]
    q_lora = q_norm_w.shape[1]
    kv_lora = kv_norm_w.shape[1]
    gn = SSM_GROUPS * SSM_STATE
    conv_dim = d_inner + 2 * gn
    n_tok = batch * seq
    n_meta_rows = batch * CHUNK
    assert d_inner == n_heads * SSM_HEAD_DIM and conv_w.shape[2] == conv_dim
    assert n_heads <= 128 and (d_inner // SSM_GROUPS // SSM_HEAD_DIM) % 2 == 0

    tm = min(1024, n_meta_rows)
    tq = 256
    assert seq % tm == 0 and n_meta_rows % tm == 0 and tm % CHUNK == 0 and seq % tq == 0

    meta_chunk = jnp.concatenate([jnp.zeros((META_PAD, d), x.dtype), meta_tokens.astype(x.dtype)], axis=0)
    h_tok = x.reshape(n_tok, d)
    h_meta = jnp.tile(meta_chunk, (batch, 1))
    meta_base_rows = 0

    o_z = 0
    o_xbc = o_z + d_inner
    o_dt = o_xbc + conv_dim
    o_cq = o_dt + n_heads
    o_ckv = o_cq + q_lora
    o_kr = o_ckv + kv_lora
    o_gs = o_kr + MLA_ROPE
    o_gm = o_gs + d
    assert o_gm + d == w_in.shape[2]
    cq_col = d_inner + conv_dim
    ckv_col = cq_col + q_lora
    gs_col = -(-(ckv_col + kv_lora) // d) * d
    gm_col = gs_col + d
    n_main = gm_col + d
    tn = 1024
    assert n_main % tn == 0 and d_inner % tn == 0 and cq_col % q_lora == 0 and ckv_col % kv_lora == 0

    tm_rope = min(512, tm)
    cos_tab, sin_tab = _rope_tables(seq, tm)
    assert HEAD_REPL * n_heads <= 128
    head_of_lane = jnp.arange(d_inner) // SSM_HEAD_DIM
    src_lane = jnp.arange(128)
    expand = ((src_lane % n_heads)[:, None] == head_of_lane[None, :]) & (src_lane < HEAD_REPL * n_heads)[:, None]
    expand = expand.astype(BF16)
    rr = jnp.arange((SSM_CONV - 1) * CHUNK)
    shift = (jnp.arange(TAIL + CHUNK)[None, :] == (TAIL + rr % CHUNK - (rr // CHUNK + 1))[:, None]).astype(BF16)

    out = None
    for i in range(depth):
        wi = w_in[i]
        w_main = jnp.concatenate([
            wi[:, o_z:o_dt], wi[:, o_cq:o_kr],
            jnp.zeros((d, gs_col - ckv_col - kv_lora), wi.dtype),
            wi[:, o_gs:]], axis=1).astype(BF16)
        w_small = jnp.concatenate([
            _spread_rope_cols(wi[:, o_kr:o_gs])] + [wi[:, o_dt:o_cq]] * HEAD_REPL + [
            jnp.zeros((d, 128 - HEAD_REPL * n_heads), wi.dtype)], axis=1).astype(BF16)
        dtb = jnp.pad(jnp.tile(dt_bias[i], HEAD_REPL), (0, 128 - HEAD_REPL * n_heads))[None]
        alog = jnp.pad(jnp.tile(a_log[i], HEAD_REPL), (0, 128 - HEAD_REPL * n_heads))[None]
        dskip = jnp.repeat(d_skip[i], SSM_HEAD_DIM)[None]
        wq = w_uq[i].reshape(q_lora, MLA_HEADS, MLA_NOPE + MLA_ROPE)
        wq = jnp.concatenate([wq[..., :MLA_NOPE], _spread_rope_cols(wq[..., MLA_NOPE:])], axis=-1)
        wq = wq.reshape(q_lora, MLA_HEADS * QK_SLAB).astype(BF16)
        wkv = w_ukv[i].reshape(kv_lora, MLA_HEADS, MLA_NOPE + MLA_V)
        wk = wkv[..., :MLA_NOPE].reshape(kv_lora, MLA_HEADS * MLA_NOPE).astype(BF16)
        wvt = wkv[..., MLA_NOPE:].reshape(kv_lora, MLA_HEADS * MLA_V).T.astype(BF16)

        proj, small = _in_proj(h_tok, h_meta, norm_mix_w[i][None], w_main, w_small, tm=tm_rope, tn=tn,
                               n_tok=n_tok, n_meta=n_meta_rows, meta_base_rows=meta_base_rows, silu_cols=d_inner)
        ys_tok, ys_meta = _ssd(proj, small, shift, conv_w[i].astype(F32), conv_b[i][None], dtb, alog, dskip,
                               ssm_norm_w[i][None], expand, batch=batch, seq=seq, n_tok=n_tok, d_inner=d_inner,
                               n_heads=n_heads, chunks_per_step=min(8, seq // CHUNK))
        q, k, vt = _mla_proj(proj, small, q_norm_w[i][None], kv_norm_w[i][None], wq, wk, wvt, cos_tab, sin_tab,
                             tm=tm, seq=seq, n_tok=n_tok, cq_col=cq_col, ckv_col=ckv_col,
                             q_lora=q_lora, kv_lora=kv_lora)
        y_tok, y_meta = _attention(q, k, vt, batch=batch, seq=seq, n_tok=n_tok, tq=tq, heads_per_step=2)
        last = i == depth - 1
        h = _mix(ys_tok, ys_meta, y_tok, y_meta, proj, h_tok, h_meta, w_branch_ssm[i].astype(BF16),
                 w_branch_mla[i].astype(BF16), w_out[i].astype(BF16), tm=tm_rope, n_tok=n_tok, n_meta=n_meta_rows,
                 rows=n_tok if last else n_tok + n_meta_rows,
                 meta_base_rows=meta_base_rows, gs_col=gs_col, gm_col=gm_col)
        if last:
            out = _mlp(h, norm_mlp_w[i][None], w_mlp_up[i].astype(BF16), w_mlp_down[i].astype(BF16),
                       final_norm_w[None], tm=tm, rows=n_tok, final=True)
        else:
            h = _mlp(h, norm_mlp_w[i][None], w_mlp_up[i].astype(BF16), w_mlp_down[i].astype(BF16),
                     final_norm_w[None], tm=tm, rows=h.shape[0], final=False)
            h_tok, h_meta, meta_base_rows = h, h, n_tok
    return out.reshape(batch, seq, d)
```

```python
import functools

import jax
import jax.numpy as jnp
import numpy as np
from jax import lax
from jax.experimental import pallas as pl
from jax.experimental.pallas import tpu as pltpu

F32 = jnp.float32
BF16 = jnp.bfloat16

N_META = 16
EPS = 1e-6
CHUNK = 128
META_PAD = CHUNK - N_META
SSM_HEAD_DIM = 64
SSM_GROUPS = 4
SSM_STATE = 128
SSM_CONV = 4
MLA_HEADS = 8
MLA_NOPE = 128
MLA_ROPE = 64
MLA_V = 128
ROPE_THETA = 10000.0
QK_SLAB = 256
MASK_NEG = -1e30
V7X_VMEM_LIMIT = 56 * 1024 * 1024
LOG2E = float(np.log2(np.e))


def _rms(x, w):
    var = jnp.mean(x * x, axis=-1, keepdims=True)
    return x * lax.rsqrt(var + EPS) * w


def _sigmoid(x):
    return 1.0 / (1.0 + jnp.exp(-x))


def _split3(x):
    hi = x.astype(BF16)
    r1 = x - hi.astype(F32)
    mid = r1.astype(BF16)
    lo = (r1 - mid.astype(F32)).astype(BF16)
    return hi, mid, lo


def _dot(a, b):
    return jnp.dot(a, b, preferred_element_type=F32)


def _dot_nt(a, b):
    return lax.dot_general(a, b, (((1,), (1,)), ((), ())), preferred_element_type=F32)


def _dot_tn(a, b):
    return lax.dot_general(a, b, (((0,), (0,)), ((), ())), preferred_element_type=F32)


def _split_rows(n_tok, n_meta, tm, meta_base_rows):
    n_tok_tiles = n_tok // tm
    n_meta_tiles = n_meta // tm
    base = meta_base_rows // tm
    tok = lambda i: jnp.minimum(i, n_tok_tiles - 1)
    meta = lambda i: base + jnp.clip(i - n_tok_tiles, 0, n_meta_tiles - 1)
    return n_tok_tiles, tok, meta


def _layer_weight_spec(w, layer, block_cols=None):
    cols = w.shape[2] if block_cols is None else block_cols
    return pl.BlockSpec((None, w.shape[1], cols), lambda *_: (layer, 0, 0), pipeline_mode=pl.Buffered(1))


def _in_proj_body(ht_ref, hm_ref, nw_ref, wa_ref, wb_ref, ws_ref, o_ref, os_ref, *, n_tok_tiles, silu_cols, tn):
    is_meta = pl.program_id(0) >= n_tok_tiles
    h = jnp.where(is_meta, hm_ref[...], ht_ref[...])
    u = _rms(h, nw_ref[...]).astype(BF16)
    os_ref[...] = _dot(u, ws_ref[...])
    n_a = wa_ref.shape[1]
    for c0 in range(0, n_a, tn):
        acc = _dot(u, wa_ref[:, c0:c0 + tn])
        if c0 < silu_cols:
            acc = acc / (1.0 + jnp.exp2(acc * (-LOG2E)))
        o_ref[:, c0:c0 + tn] = acc.astype(o_ref.dtype)
    for c0 in range(0, wb_ref.shape[1], tn):
        o_ref[:, n_a + c0:n_a + c0 + tn] = _dot(u, wb_ref[:, c0:c0 + tn]).astype(o_ref.dtype)


def _in_proj(h_tok, h_meta, nw, w_in_b, layer, n_lead, w_tail, w_small, *, tm, tn, n_tok, n_meta, meta_base_rows,
             silu_cols):
    d = h_tok.shape[1]
    m = n_tok + n_meta
    n = n_lead + w_tail.shape[1]
    ns = w_small.shape[1]
    n_tok_tiles, tok, meta = _split_rows(n_tok, n_meta, tm, meta_base_rows)
    single = pl.Buffered(1)
    return pl.pallas_call(
        functools.partial(_in_proj_body, n_tok_tiles=n_tok_tiles, silu_cols=silu_cols, tn=tn),
        grid=(m // tm,),
        in_specs=[
            pl.BlockSpec((tm, d), lambda i: (tok(i), 0)),
            pl.BlockSpec((tm, d), lambda i: (meta(i), 0)),
            pl.BlockSpec((1, d), lambda i: (0, 0)),
            _layer_weight_spec(w_in_b, layer, n_lead),
            pl.BlockSpec(w_tail.shape, lambda i: (0, 0), pipeline_mode=single),
            pl.BlockSpec((d, ns), lambda i: (0, 0), pipeline_mode=single),
        ],
        out_specs=[
            pl.BlockSpec((tm, n), lambda i: (i, 0)),
            pl.BlockSpec((tm, ns), lambda i: (i, 0)),
        ],
        out_shape=[
            jax.ShapeDtypeStruct((m, n), BF16),
            jax.ShapeDtypeStruct((m, ns), F32),
        ],
        compiler_params=pltpu.CompilerParams(
            dimension_semantics=("parallel",),
            vmem_limit_bytes=V7X_VMEM_LIMIT),
        name="in_proj",
    )(h_tok, h_meta, nw, w_in_b, w_tail, w_small)


TAIL = 16
HEAD_REPL = 3


def _conv_silu(raw, tail_ref, c0, c1, shift, w, b):
    x_ext = jnp.concatenate([tail_ref[:, c0:c1], raw], axis=0)
    tail_ref[:, c0:c1] = raw[CHUNK - TAIL:, :]
    delayed = _dot(shift, x_ext)
    acc = b[:, c0:c1] + w[SSM_CONV - 1:SSM_CONV, c0:c1] * raw.astype(F32)
    for s in range(1, SSM_CONV):
        acc = acc + w[SSM_CONV - 1 - s:SSM_CONV - s, c0:c1] * delayed[(s - 1) * CHUNK:s * CHUNK]
    return acc / (1.0 + jnp.exp2(acc * (-LOG2E)))


def _ssd_chunk(sz, x_raw, b_raw, c_raw, dt_raw, shift_ref, cw_ref, cb_ref, dtb_ref, alog_ref, dskip_ref,
               nw_ref, expand_ref, state_ref, tail_ref, y_store, *, meta, n_heads):
    d_inner = x_raw.shape[1]
    gn = b_raw.shape[1]
    gw = d_inner // SSM_GROUPS
    heads_per_group = gw // SSM_HEAD_DIM

    shift = shift_ref[...]
    cw = cw_ref[...]
    cb = cb_ref[...]
    xs = _conv_silu(x_raw, tail_ref, 0, d_inner, shift, cw, cb)
    bm = _conv_silu(b_raw, tail_ref, d_inner, d_inner + gn, shift, cw, cb).astype(BF16)
    cm = _conv_silu(c_raw, tail_ref, d_inner + gn, d_inner + 2 * gn, shift, cw, cb).astype(BF16)

    xdt = dt_raw + dtb_ref[...]
    dt = jnp.maximum(xdt, 0.0) + jnp.log(1.0 + jnp.exp(-jnp.abs(xdt)))
    if meta:
        row = lax.broadcasted_iota(jnp.int32, (CHUNK, 1), 0)
        dt = jnp.where(row >= META_PAD, dt, 0.0)
    adt2 = dt * (-LOG2E * jnp.exp(alog_ref[...]))

    ri = lax.broadcasted_iota(jnp.int32, (CHUNK, CHUNK), 0)
    ci = lax.broadcasted_iota(jnp.int32, (CHUNK, CHUNK), 1)
    causal = ri >= ci
    tri = causal.astype(BF16)
    hi, mid, lo = _split3(adt2)
    a_cs = _dot(tri, hi) + _dot(tri, mid) + _dot(tri, lo)
    row_t = (jnp.log2(dt) - a_cs).T
    exp_a = jnp.exp2(a_cs)
    w_state = jnp.exp2(a_cs[CHUNK - 1:CHUNK] - a_cs) * dt

    grp = lax.broadcasted_iota(jnp.int32, (1, 128), 1) // n_heads
    both = jnp.concatenate([w_state, exp_a], axis=0)
    p0, p1, p2 = _split3(both)
    lhs = jnp.where(grp == 0, p0, jnp.where(grp == 1, p1, p2))
    expanded = _dot(lhs, expand_ref[...])
    w_exp = expanded[0:CHUNK]
    exp_a_exp = expanded[CHUNK:]

    xs_b = xs.astype(BF16)
    xw = (xs * w_exp).astype(BF16)
    lane = lax.broadcasted_iota(jnp.int32, (1, 2 * SSM_HEAD_DIM), 1)
    lo_half = lane < SSM_HEAD_DIM

    def intra(g):
        bm_g = bm[:, g * SSM_STATE:(g + 1) * SSM_STATE]
        cm_g = cm[:, g * SSM_STATE:(g + 1) * SSM_STATE]
        cbm = _dot_nt(cm_g, bm_g).astype(BF16)
        y_pairs = []
        for pj in range(heads_per_group // 2):
            ms = []
            for e in range(2):
                hd = g * heads_per_group + 2 * pj + e
                seg = a_cs[:, hd:hd + 1] + row_t[hd:hd + 1, :]
                decay = jnp.exp2(jnp.where(causal, seg, MASK_NEG))
                ms.append(cbm * decay.astype(BF16))
            lhs_p = jnp.concatenate(ms, axis=1)
            p_off = g * gw + pj * 2 * SSM_HEAD_DIM
            xpair = xs_b[:, p_off:p_off + 2 * SSM_HEAD_DIM]
            zero = jnp.zeros_like(xpair)
            rhs = jnp.concatenate([jnp.where(lo_half, xpair, zero),
                                   jnp.where(lo_half, zero, xpair)], axis=0)
            y_pairs.append(_dot(lhs_p, rhs))
        return jnp.concatenate(y_pairs, axis=1)

    def finish(g, y_diag):
        gs = slice(g * gw, (g + 1) * gw)
        bm_g = bm[:, g * SSM_STATE:(g + 1) * SSM_STATE]
        cm_g = cm[:, g * SSM_STATE:(g + 1) * SSM_STATE]
        prev = state_ref[:, gs]
        y_off = _dot(cm_g, prev.astype(BF16)) * exp_a_exp[:, gs]
        state_ref[:, gs] = prev * exp_a_exp[CHUNK - 1:CHUNK, gs] + _dot_tn(bm_g, xw[:, gs])
        y = (y_diag + y_off + xs[:, gs] * dskip_ref[:, gs]) * sz[:, gs].astype(F32)
        y_store(gs, _rms(y, nw_ref[:, gs]).astype(BF16))

    y_next = intra(0)
    for g in range(SSM_GROUPS):
        y_cur = y_next
        if g + 1 < SSM_GROUPS:
            y_next = intra(g + 1)
        finish(g, y_cur)


def _ssd_body(szt_ref, xt_ref, bt_ref, ct_ref, dtt_ref, szm_ref, xm_ref, bm_ref, cm_ref, dtm_ref,
              shift_ref, cw_ref, cb_ref, dtb_ref, alog_ref, dskip_ref, nw_ref, expand_ref,
              yt_ref, ym_ref, state_ref, tail_ref, *, chunks_per_step, n_heads):
    consts = (shift_ref, cw_ref, cb_ref, dtb_ref, alog_ref, dskip_ref, nw_ref, expand_ref, state_ref, tail_ref)

    @pl.when(pl.program_id(1) == 0)
    def _():
        state_ref[...] = jnp.zeros_like(state_ref)
        tail_ref[...] = jnp.zeros_like(tail_ref)

        def store_meta(cols, val):
            ym_ref[:, cols] = val

        _ssd_chunk(szm_ref[...], xm_ref[...], bm_ref[...], cm_ref[...], dtm_ref[...], *consts,
                   store_meta, meta=True, n_heads=n_heads)

    def body(c, carry):
        rows = pl.ds(pl.multiple_of(c * CHUNK, CHUNK), CHUNK)

        def store_tok(cols, val):
            yt_ref[rows, cols] = val

        _ssd_chunk(szt_ref[rows, :], xt_ref[rows, :], bt_ref[rows, :], ct_ref[rows, :], dtt_ref[rows, :],
                   *consts, store_tok, meta=False, n_heads=n_heads)
        return carry

    lax.fori_loop(0, chunks_per_step, body, 0)


def _ssd(proj, small, shift, conv_w, conv_b, dt_bias, a_log, d_skip, norm_w, expand, *, batch, seq, n_tok, d_inner,
         n_heads, chunks_per_step):
    gn = SSM_GROUPS * SSM_STATE
    conv_dim = d_inner + 2 * gn
    rows = chunks_per_step * CHUNK
    steps = seq // rows
    meta0 = n_tok // CHUNK

    def tok(width_blk):
        return lambda b, s: (b * steps + s, width_blk)

    def meta(width_blk):
        return lambda b, s: (meta0 + b, width_blk)

    const = lambda b, s: (0, 0)
    bc_blk = 2 * d_inner // gn
    return pl.pallas_call(
        functools.partial(_ssd_body, chunks_per_step=chunks_per_step, n_heads=n_heads),
        grid=(batch, steps),
        in_specs=[
            pl.BlockSpec((rows, d_inner), tok(0)),
            pl.BlockSpec((rows, d_inner), tok(1)),
            pl.BlockSpec((rows, gn), tok(bc_blk)),
            pl.BlockSpec((rows, gn), tok(bc_blk + 1)),
            pl.BlockSpec((rows, 128), tok(1)),
            pl.BlockSpec((CHUNK, d_inner), meta(0)),
            pl.BlockSpec((CHUNK, d_inner), meta(1)),
            pl.BlockSpec((CHUNK, gn), meta(bc_blk)),
            pl.BlockSpec((CHUNK, gn), meta(bc_blk + 1)),
            pl.BlockSpec((CHUNK, 128), meta(1)),
            pl.BlockSpec(shift.shape, const),
            pl.BlockSpec((SSM_CONV, conv_dim), const),
            pl.BlockSpec((1, conv_dim), const),
            pl.BlockSpec((1, 128), const),
            pl.BlockSpec((1, 128), const),
            pl.BlockSpec((1, d_inner), const),
            pl.BlockSpec((1, d_inner), const),
            pl.BlockSpec(expand.shape, const),
        ],
        out_specs=[
            pl.BlockSpec((rows, d_inner), tok(0)),
            pl.BlockSpec((CHUNK, d_inner), lambda b, s: (b, 0)),
        ],
        out_shape=[
            jax.ShapeDtypeStruct((n_tok, d_inner), BF16),
            jax.ShapeDtypeStruct((batch * CHUNK, d_inner), BF16),
        ],
        scratch_shapes=[pltpu.VMEM((SSM_STATE, d_inner), F32),
                        pltpu.VMEM((TAIL, conv_dim), BF16)],
        compiler_params=pltpu.CompilerParams(
            dimension_semantics=("parallel", "arbitrary"),
            vmem_limit_bytes=V7X_VMEM_LIMIT),
        name="ssd",
    )(proj, proj, proj, proj, small, proj, proj, proj, proj, small,
      shift, conv_w, conv_b, dt_bias, a_log, d_skip, norm_w, expand)


def _rope(x, cos_t, sin_t):
    return x * cos_t + pltpu.roll(x, 64, 1) * sin_t


def _mla_proj_body(cq_ref, ckv_ref, kr_ref, qnw_ref, kvnw_ref, wq_ref, wk_ref, wvt_ref, cos_ref, sin_ref,
                   q_ref, k_ref, vt_ref, *, q_scale):
    cos_t = cos_ref[...]
    sin_t = sin_ref[...]
    cqn = _rms(cq_ref[...].astype(F32), qnw_ref[...]).astype(BF16)
    qf = _dot(cqn, wq_ref[...])
    ckvn = _rms(ckv_ref[...].astype(F32), kvnw_ref[...]).astype(BF16)
    kf = _dot(ckvn, wk_ref[...])
    vt = _dot_nt(wvt_ref[...], ckvn)
    k_pe = _rope(kr_ref[...], cos_t, sin_t).astype(BF16)
    for hd in range(MLA_HEADS):
        o = hd * QK_SLAB
        q_ref[hd, :, 0:MLA_NOPE] = (qf[:, o:o + MLA_NOPE] * q_scale).astype(BF16)
        q_ref[hd, :, MLA_NOPE:QK_SLAB] = (
            _rope(qf[:, o + MLA_NOPE:o + QK_SLAB], cos_t, sin_t) * q_scale).astype(BF16)
        k_ref[hd, :, 0:MLA_NOPE] = kf[:, hd * MLA_NOPE:(hd + 1) * MLA_NOPE].astype(BF16)
        k_ref[hd, :, MLA_NOPE:QK_SLAB] = k_pe
        vt_ref[hd] = vt[hd * MLA_V:(hd + 1) * MLA_V, :].astype(BF16)


def _mla_proj(proj, small, q_norm_w, kv_norm_w, wq, wk, wvt, cos_tab, sin_tab, *, tm, seq, n_tok,
              cq_col, ckv_col, q_lora, kv_lora):
    m = proj.shape[0]
    n_tok_tiles = n_tok // tm
    tiles_per_seq = seq // tm

    def tab(i):
        return (jnp.where(i < n_tok_tiles, i % tiles_per_seq, tiles_per_seq), 0)

    const = lambda i: (0, 0)
    q_scale = float((MLA_NOPE + MLA_ROPE) ** -0.5 * np.log2(np.e))
    return pl.pallas_call(
        functools.partial(_mla_proj_body, q_scale=q_scale),
        grid=(m // tm,),
        in_specs=[
            pl.BlockSpec((tm, q_lora), lambda i: (i, cq_col // q_lora)),
            pl.BlockSpec((tm, kv_lora), lambda i: (i, ckv_col // kv_lora)),
            pl.BlockSpec((tm, 128), lambda i: (i, 0)),
            pl.BlockSpec((1, q_lora), const),
            pl.BlockSpec((1, kv_lora), const),
            pl.BlockSpec(wq.shape, const),
            pl.BlockSpec(wk.shape, const),
            pl.BlockSpec(wvt.shape, const),
            pl.BlockSpec((tm, 128), tab),
            pl.BlockSpec((tm, 128), tab),
        ],
        out_specs=[
            pl.BlockSpec((MLA_HEADS, tm, QK_SLAB), lambda i: (0, i, 0)),
            pl.BlockSpec((MLA_HEADS, tm, QK_SLAB), lambda i: (0, i, 0)),
            pl.BlockSpec((MLA_HEADS, MLA_V, tm), lambda i: (0, 0, i)),
        ],
        out_shape=[
            jax.ShapeDtypeStruct((MLA_HEADS, m, QK_SLAB), BF16),
            jax.ShapeDtypeStruct((MLA_HEADS, m, QK_SLAB), BF16),
            jax.ShapeDtypeStruct((MLA_HEADS, MLA_V, m), BF16),
        ],
        compiler_params=pltpu.CompilerParams(
            dimension_semantics=("parallel",),
            vmem_limit_bytes=V7X_VMEM_LIMIT),
        name="mla_proj",
    )(proj, proj, small, q_norm_w, kv_norm_w, wq, wk, wvt, cos_tab, sin_tab)


ONES_ROWS = 16


def _softmax_t(s_t, n_keys, masks):
    pieces = []
    pos = 0
    for r0, r1, keep in masks:
        if r0 > pos:
            pieces.append(s_t[pos:r0])
        pieces.append(jnp.where(keep, s_t[r0:r1], MASK_NEG))
        pos = r1
    if pos < n_keys:
        pieces.append(s_t[pos:n_keys])
    m = functools.reduce(jnp.maximum, [jnp.max(p, axis=0, keepdims=True) for p in pieces])
    return jnp.concatenate([jnp.exp2(p - m).astype(BF16) for p in pieces], axis=0)


def _pv_t(p_t, vx_ref, n_keys):
    o_t = _dot(vx_ref[:, 0:n_keys], p_t)
    o_t = o_t[0:MLA_V] * (1.0 / o_t[MLA_V:MLA_V + 1])
    return o_t.T


def _attn_body(qt_ref, qm_ref, kt_ref, km_ref, vt_ref, vm_ref, ot_ref, om_ref, kk_ref, vx_ref, *, tq):
    heads, seq = qt_ref.shape[0], qt_ref.shape[1]
    key_ok = lax.broadcasted_iota(jnp.int32, (CHUNK, 1), 0) >= META_PAD
    rm = lax.broadcasted_iota(jnp.int32, (CHUNK, CHUNK), 0)
    cm = lax.broadcasted_iota(jnp.int32, (CHUNK, CHUNK), 1)
    rq = lax.broadcasted_iota(jnp.int32, (tq, tq), 0)
    cq = lax.broadcasted_iota(jnp.int32, (tq, tq), 1)
    causal_t = rq <= cq

    blocks = []
    for hh in range(heads):
        kk_ref[hh, 0:CHUNK, :] = km_ref[hh]
        kk_ref[hh, CHUNK:, :] = kt_ref[hh]
        vx_ref[hh, 0:MLA_V, 0:CHUNK] = vm_ref[hh]
        vx_ref[hh, 0:MLA_V, CHUNK:] = vt_ref[hh]
        vx_ref[hh, MLA_V:, :] = jnp.ones((ONES_ROWS, seq + CHUNK), BF16)
        meta_block = (hh, qm_ref, 0, CHUNK, CHUNK, [(0, CHUNK, (rm <= cm) & (rm >= META_PAD))], om_ref)
        tok_blocks = []
        for qs in range(0, seq, tq):
            n_keys = CHUNK + qs + tq
            tok_blocks.append((hh, qt_ref, qs, tq, n_keys,
                               [(0, CHUNK, key_ok), (n_keys - tq, n_keys, causal_t)], ot_ref))
        blocks += tok_blocks[:1] + tok_blocks[:0:-1] + [meta_block]

    def scores(blk):
        hh, q_ref, qs, rows, n_keys, _, _ = blk
        return _dot_nt(kk_ref[hh, 0:n_keys, :], q_ref[hh, qs:qs + rows, :])

    def finish(blk, p_t):
        hh, _, qs, rows, n_keys, _, o_ref = blk
        o_ref[hh, qs:qs + rows, :] = _pv_t(p_t, vx_ref.at[hh], n_keys).astype(o_ref.dtype)

    s_next = scores(blocks[0])
    p_prev = None
    for bi, blk in enumerate(blocks):
        s_cur = s_next
        if bi + 1 < len(blocks):
            s_next = scores(blocks[bi + 1])
        if p_prev is not None:
            finish(blocks[bi - 1], p_prev)
        p_prev = _softmax_t(s_cur, blk[4], blk[5])
    finish(blocks[-1], p_prev)


def _attention(q, k, vt, *, batch, seq, n_tok, tq, heads_per_step):
    lp = seq + CHUNK
    meta0 = n_tok // CHUNK
    hps = heads_per_step
    tok_rows = lambda b, h: (h, b, 0)
    meta_rows = lambda b, h: (h, meta0 + b, 0)
    return pl.pallas_call(
        functools.partial(_attn_body, tq=tq),
        grid=(batch, MLA_HEADS // hps),
        in_specs=[
            pl.BlockSpec((hps, seq, QK_SLAB), tok_rows),
            pl.BlockSpec((hps, CHUNK, QK_SLAB), meta_rows),
            pl.BlockSpec((hps, seq, QK_SLAB), tok_rows),
            pl.BlockSpec((hps, CHUNK, QK_SLAB), meta_rows),
            pl.BlockSpec((hps, MLA_V, seq), lambda b, h: (h, 0, b)),
            pl.BlockSpec((hps, MLA_V, CHUNK), lambda b, h: (h, 0, meta0 + b)),
        ],
        out_specs=[
            pl.BlockSpec((hps, seq, MLA_V), tok_rows),
            pl.BlockSpec((hps, CHUNK, MLA_V), tok_rows),
        ],
        out_shape=[
            jax.ShapeDtypeStruct((MLA_HEADS, n_tok, MLA_V), BF16),
            jax.ShapeDtypeStruct((MLA_HEADS, batch * CHUNK, MLA_V), BF16),
        ],
        scratch_shapes=[pltpu.VMEM((hps, lp, QK_SLAB), BF16),
                        pltpu.VMEM((hps, MLA_V + ONES_ROWS, lp), BF16)],
        compiler_params=pltpu.CompilerParams(
            dimension_semantics=("parallel", "parallel"),
            vmem_limit_bytes=V7X_VMEM_LIMIT),
        name="mla_attention",
    )(q, q, k, k, vt, vt)


def _mix_body(yst_ref, ysm_ref, ymt_ref, ymm_ref, gs_ref, gm_ref, ht_ref, hm_ref, wbs_ref, wbm_ref, wo_ref,
              o_ref, *, n_tok_tiles):
    i = pl.program_id(0)
    tm = o_ref.shape[0]
    is_meta = i >= n_tok_tiles
    ymt = jnp.concatenate([ymt_ref[hd] for hd in range(MLA_HEADS)], axis=1)
    ymm = jnp.concatenate([ymm_ref[hd] for hd in range(MLA_HEADS)], axis=1)
    ym = jnp.where(is_meta, ymm, ymt)
    h = jnp.where(is_meta, hm_ref[...], ht_ref[...])
    ys = jnp.where(is_meta, ysm_ref[...], yst_ref[...])
    a = _dot(ys, wbs_ref[...])
    b = _dot(ym, wbm_ref[...])
    mixed = _sigmoid(gs_ref[...].astype(F32)) * a + _sigmoid(gm_ref[...].astype(F32)) * b
    hn = h + _dot(mixed.astype(BF16), wo_ref[...])
    row = lax.broadcasted_iota(jnp.int32, (tm, 1), 0)
    inert = is_meta & ((row & (CHUNK - 1)) < META_PAD)
    o_ref[...] = jnp.where(inert, 0.0, hn)


def _mix(y_ssm_tok, y_ssm_meta, y_mla_tok, y_mla_meta, proj, h_tok, h_meta, w_bs, w_bm, w_o, layer, *, tm, n_tok,
         n_meta, rows, meta_base_rows, gs_col, gm_col):
    d = h_tok.shape[1]
    m = rows
    n_tok_tiles, tok, meta = _split_rows(n_tok, n_meta, tm, meta_base_rows)
    _, _, meta0 = _split_rows(n_tok, n_meta, tm, 0)
    return pl.pallas_call(
        functools.partial(_mix_body, n_tok_tiles=n_tok_tiles),
        grid=(m // tm,),
        in_specs=[
            pl.BlockSpec((tm, y_ssm_tok.shape[1]), lambda i: (tok(i), 0)),
            pl.BlockSpec((tm, y_ssm_tok.shape[1]), lambda i: (meta0(i), 0)),
            pl.BlockSpec((MLA_HEADS, tm, MLA_V), lambda i: (0, tok(i), 0)),
            pl.BlockSpec((MLA_HEADS, tm, MLA_V), lambda i: (0, meta0(i), 0)),
            pl.BlockSpec((tm, d), lambda i: (i, gs_col // d)),
            pl.BlockSpec((tm, d), lambda i: (i, gm_col // d)),
            pl.BlockSpec((tm, d), lambda i: (tok(i), 0)),
            pl.BlockSpec((tm, d), lambda i: (meta(i), 0)),
            _layer_weight_spec(w_bs, layer),
            _layer_weight_spec(w_bm, layer),
            _layer_weight_spec(w_o, layer),
        ],
        out_specs=pl.BlockSpec((tm, d), lambda i: (i, 0)),
        out_shape=jax.ShapeDtypeStruct((m, d), F32),
        compiler_params=pltpu.CompilerParams(
            dimension_semantics=("parallel",),
            vmem_limit_bytes=V7X_VMEM_LIMIT),
        name="mix_out",
    )(y_ssm_tok, y_ssm_meta, y_mla_tok, y_mla_meta, proj, proj, h_tok, h_meta, w_bs, w_bm, w_o)


def _mlp_body(h_ref, nw_ref, wu_ref, wd_ref, fnw_ref, o_ref, *, ff_chunk, final):
    h = h_ref[...]
    v = _rms(h, nw_ref[...]).astype(BF16)
    d_ff = wu_ref.shape[1]
    acc = jnp.zeros_like(h)
    for f0 in range(0, d_ff, ff_chunk):
        a = _dot(v, wu_ref[:, f0:f0 + ff_chunk])
        a = jnp.square(jnp.maximum(a, 0.0)).astype(BF16)
        acc = acc + _dot(a, wd_ref[f0:f0 + ff_chunk, :])
    hn = h + acc
    if final:
        hn = _rms(hn, fnw_ref[...])
    o_ref[...] = hn


def _mlp(h, nw, w_up, w_down, layer, final_nw, *, tm, rows, final):
    d = h.shape[1]
    const = lambda i: (0, 0)
    return pl.pallas_call(
        functools.partial(_mlp_body, ff_chunk=1024, final=final),
        grid=(rows // tm,),
        in_specs=[
            pl.BlockSpec((tm, d), lambda i: (i, 0)),
            pl.BlockSpec((1, d), const),
            _layer_weight_spec(w_up, layer),
            _layer_weight_spec(w_down, layer),
            pl.BlockSpec((1, d), const),
        ],
        out_specs=pl.BlockSpec((tm, d), lambda i: (i, 0)),
        out_shape=jax.ShapeDtypeStruct((rows, d), F32),
        compiler_params=pltpu.CompilerParams(
            dimension_semantics=("parallel",),
            vmem_limit_bytes=V7X_VMEM_LIMIT),
        name="mlp_final" if final else "mlp",
    )(h, nw, w_up, w_down, final_nw)


def _rope_tables(seq, tm):
    half = MLA_ROPE // 2
    f32 = np.float32
    inv = np.power(f32(ROPE_THETA), -np.arange(0, MLA_ROPE, 2, dtype=f32) / f32(MLA_ROPE)).astype(f32)
    tok_pos = np.arange(N_META, N_META + seq, dtype=f32)
    meta_pos = np.maximum(np.arange(CHUNK, dtype=f32) - f32(META_PAD), f32(0.0))
    pos = np.concatenate([tok_pos, np.tile(meta_pos, tm // CHUNK)])
    ang = (pos[:, None] * inv[None, :]).astype(f32)
    cos, sin = np.cos(ang).astype(f32), np.sin(ang).astype(f32)
    zero = np.zeros_like(cos)
    assert 4 * half == 128
    return (jnp.asarray(np.concatenate([cos, zero, cos, zero], axis=1)),
            jnp.asarray(np.concatenate([-sin, zero, sin, zero], axis=1)))


def _spread_rope_cols(w):
    half = MLA_ROPE // 2
    zero = jnp.zeros(w.shape[:-1] + (half,), w.dtype)
    return jnp.concatenate([w[..., :half], zero, w[..., half:], zero], axis=-1)


def kernel(x, meta_tokens, norm_mix_w, w_in, conv_w, conv_b, dt_bias, a_log, d_skip, ssm_norm_w,
           q_norm_w, kv_norm_w, w_uq, w_ukv, w_branch_ssm, w_branch_mla, w_out, norm_mlp_w,
           w_mlp_up, w_mlp_down, final_norm_w):
    batch, seq, d = x.shape
    depth = w_in.shape[0]
    d_inner = w_branch_ssm.shape[1]
    n_heads = dt_bias.shape[1]
    q_lora = q_norm_w.shape[1]
    kv_lora = kv_norm_w.shape[1]
    gn = SSM_GROUPS * SSM_STATE
    conv_dim = d_inner + 2 * gn
    n_tok = batch * seq
    n_meta_rows = batch * CHUNK
    assert d_inner == n_heads * SSM_HEAD_DIM and conv_w.shape[2] == conv_dim
    assert n_heads <= 128 and (d_inner // SSM_GROUPS // SSM_HEAD_DIM) % 2 == 0

    tm = min(1024, n_meta_rows)
    tq = 256
    assert seq % tm == 0 and n_meta_rows % tm == 0 and tm % CHUNK == 0 and seq % tq == 0

    meta_chunk = jnp.concatenate([jnp.zeros((META_PAD, d), x.dtype), meta_tokens.astype(x.dtype)], axis=0)
    h_tok = x.reshape(n_tok, d)
    h_meta = jnp.tile(meta_chunk, (batch, 1))
    meta_base_rows = 0

    o_z = 0
    o_xbc = o_z + d_inner
    o_dt = o_xbc + conv_dim
    o_cq = o_dt + n_heads
    o_ckv = o_cq + q_lora
    o_kr = o_ckv + kv_lora
    o_gs = o_kr + MLA_ROPE
    o_gm = o_gs + d
    assert o_gm + d == w_in.shape[2]
    cq_col = d_inner + conv_dim
    ckv_col = cq_col + q_lora
    gs_col = -(-(ckv_col + kv_lora) // d) * d
    gm_col = gs_col + d
    n_main = gm_col + d
    tn = 1024
    assert n_main % tn == 0 and d_inner % tn == 0 and cq_col % q_lora == 0 and ckv_col % kv_lora == 0

    tm_rope = min(512, tm)
    cos_tab, sin_tab = _rope_tables(seq, tm)
    assert HEAD_REPL * n_heads <= 128
    head_of_lane = np.arange(d_inner) // SSM_HEAD_DIM
    src_lane = np.arange(128)
    expand = ((src_lane % n_heads)[:, None] == head_of_lane[None, :]) & (src_lane < HEAD_REPL * n_heads)[:, None]
    expand = jnp.asarray(expand, BF16)
    rr = np.arange((SSM_CONV - 1) * CHUNK)
    shift = jnp.asarray(np.arange(TAIL + CHUNK)[None, :] == (TAIL + rr % CHUNK - (rr // CHUNK + 1))[:, None], BF16)

    w_in_b = w_in.astype(BF16)
    w_bs_b = w_branch_ssm.astype(BF16)
    w_bm_b = w_branch_mla.astype(BF16)
    w_out_b = w_out.astype(BF16)
    w_up_b = w_mlp_up.astype(BF16)
    w_down_b = w_mlp_down.astype(BF16)
    assert o_dt % tn == 0 and cq_col == o_dt

    out = None
    for i in range(depth):
        wi = w_in_b[i]
        w_tail = jnp.concatenate([
            wi[:, o_cq:o_kr], jnp.zeros((d, gs_col - ckv_col - kv_lora), wi.dtype), wi[:, o_gs:]], axis=1)
        w_small = jnp.concatenate([
            _spread_rope_cols(wi[:, o_kr:o_gs])] + [wi[:, o_dt:o_cq]] * HEAD_REPL + [
            jnp.zeros((d, 128 - HEAD_REPL * n_heads), wi.dtype)], axis=1)
        dtb = jnp.pad(jnp.tile(dt_bias[i], HEAD_REPL), (0, 128 - HEAD_REPL * n_heads))[None]
        alog = jnp.pad(jnp.tile(a_log[i], HEAD_REPL), (0, 128 - HEAD_REPL * n_heads))[None]
        dskip = jnp.repeat(d_skip[i], SSM_HEAD_DIM)[None]
        wq = w_uq[i].reshape(q_lora, MLA_HEADS, MLA_NOPE + MLA_ROPE)
        wq = jnp.concatenate([wq[..., :MLA_NOPE], _spread_rope_cols(wq[..., MLA_NOPE:])], axis=-1)
        wq = wq.reshape(q_lora, MLA_HEADS * QK_SLAB).astype(BF16)
        wkv = w_ukv[i].reshape(kv_lora, MLA_HEADS, MLA_NOPE + MLA_V)
        wk = wkv[..., :MLA_NOPE].reshape(kv_lora, MLA_HEADS * MLA_NOPE).astype(BF16)
        wvt = wkv[..., MLA_NOPE:].reshape(kv_lora, MLA_HEADS * MLA_V).T.astype(BF16)

        proj, small = _in_proj(h_tok, h_meta, norm_mix_w[i][None], w_in_b, i, o_dt, w_tail, w_small, tm=tm_rope,
                               tn=tn, n_tok=n_tok, n_meta=n_meta_rows, meta_base_rows=meta_base_rows,
                               silu_cols=d_inner)
        ys_tok, ys_meta = _ssd(proj, small, shift, conv_w[i].astype(F32), conv_b[i][None], dtb, alog, dskip,
                               ssm_norm_w[i][None], expand, batch=batch, seq=seq, n_tok=n_tok, d_inner=d_inner,
                               n_heads=n_heads, chunks_per_step=min(8, seq // CHUNK))
        q, k, vt = _mla_proj(proj, small, q_norm_w[i][None], kv_norm_w[i][None], wq, wk, wvt, cos_tab, sin_tab,
                             tm=tm, seq=seq, n_tok=n_tok, cq_col=cq_col, ckv_col=ckv_col,
                             q_lora=q_lora, kv_lora=kv_lora)
        y_tok, y_meta = _attention(q, k, vt, batch=batch, seq=seq, n_tok=n_tok, tq=tq, heads_per_step=4)
        last = i == depth - 1
        h = _mix(ys_tok, ys_meta, y_tok, y_meta, proj, h_tok, h_meta, w_bs_b, w_bm_b, w_out_b, i, tm=tm_rope,
                 n_tok=n_tok, n_meta=n_meta_rows, rows=n_tok if last else n_tok + n_meta_rows,
                 meta_base_rows=meta_base_rows, gs_col=gs_col, gm_col=gm_col)
        if last:
            out = _mlp(h, norm_mlp_w[i][None], w_up_b, w_down_b, i, final_norm_w[None], tm=tm, rows=n_tok,
                       final=True)
        else:
            h = _mlp(h, norm_mlp_w[i][None], w_up_b, w_down_b, i, final_norm_w[None], tm=tm, rows=h.shape[0],
                     final=False)
            h_tok, h_meta, meta_base_rows = h, h, n_tok
    return out.reshape(batch, seq, d)
```

```python
import functools

import jax
import jax.numpy as jnp
import numpy as np
from jax import lax
from jax.experimental import pallas as pl
from jax.experimental.pallas import tpu as pltpu

F32 = jnp.float32
BF16 = jnp.bfloat16

N_META = 16
EPS = 1e-6
CHUNK = 128
META_PAD = CHUNK - N_META
SSM_HEAD_DIM = 64
SSM_GROUPS = 4
SSM_STATE = 128
SSM_CONV = 4
MLA_HEADS = 8
MLA_NOPE = 128
MLA_ROPE = 64
MLA_V = 128
ROPE_THETA = 10000.0
QK_SLAB = 256
MASK_NEG = -1e30
V7X_VMEM_LIMIT = 56 * 1024 * 1024
LOG2E = float(np.log2(np.e))


def _rms(x, w):
    var = jnp.mean(x * x, axis=-1, keepdims=True)
    return x * lax.rsqrt(var + EPS) * w


def _sigmoid(x):
    return 1.0 / (1.0 + jnp.exp(-x))


def _split3(x):
    hi = x.astype(BF16)
    r1 = x - hi.astype(F32)
    mid = r1.astype(BF16)
    lo = (r1 - mid.astype(F32)).astype(BF16)
    return hi, mid, lo


def _dot(a, b):
    return jnp.dot(a, b, preferred_element_type=F32)


def _dot_nt(a, b):
    return lax.dot_general(a, b, (((1,), (1,)), ((), ())), preferred_element_type=F32)


def _dot_tn(a, b):
    return lax.dot_general(a, b, (((0,), (0,)), ((), ())), preferred_element_type=F32)


def _split_rows(n_tok, n_meta, tm, meta_base_rows):
    n_tok_tiles = n_tok // tm
    n_meta_tiles = n_meta // tm
    base = meta_base_rows // tm
    tok = lambda i: jnp.minimum(i, n_tok_tiles - 1)
    meta = lambda i: base + jnp.clip(i - n_tok_tiles, 0, n_meta_tiles - 1)
    return n_tok_tiles, tok, meta


def _layer_weight_spec(w, layer, block_cols=None):
    cols = w.shape[2] if block_cols is None else block_cols
    return pl.BlockSpec((None, w.shape[1], cols), lambda *_: (layer, 0, 0), pipeline_mode=pl.Buffered(1))


def _in_proj_body(ht_ref, hm_ref, nw_ref, wa_ref, wb_ref, ws_ref, o_ref, os_ref, *, n_tok_tiles, silu_cols, tn):
    is_meta = pl.program_id(0) >= n_tok_tiles
    h = jnp.where(is_meta, hm_ref[...], ht_ref[...])
    u = _rms(h, nw_ref[...]).astype(BF16)
    os_ref[...] = _dot(u, ws_ref[...])
    n_a = wa_ref.shape[1]
    for c0 in range(0, n_a, tn):
        acc = _dot(u, wa_ref[:, c0:c0 + tn])
        if c0 < silu_cols:
            acc = acc / (1.0 + jnp.exp2(acc * (-LOG2E)))
        o_ref[:, c0:c0 + tn] = acc.astype(o_ref.dtype)
    for c0 in range(0, wb_ref.shape[1], tn):
        o_ref[:, n_a + c0:n_a + c0 + tn] = _dot(u, wb_ref[:, c0:c0 + tn]).astype(o_ref.dtype)


def _in_proj(h_tok, h_meta, nw, w_in_b, layer, n_lead, w_tail, w_small, *, tm, tn, n_tok, n_meta, meta_base_rows,
             silu_cols):
    d = h_tok.shape[1]
    m = n_tok + n_meta
    n = n_lead + w_tail.shape[1]
    ns = w_small.shape[1]
    n_tok_tiles, tok, meta = _split_rows(n_tok, n_meta, tm, meta_base_rows)
    single = pl.Buffered(1)
    return pl.pallas_call(
        functools.partial(_in_proj_body, n_tok_tiles=n_tok_tiles, silu_cols=silu_cols, tn=tn),
        grid=(m // tm,),
        in_specs=[
            pl.BlockSpec((tm, d), lambda i: (tok(i), 0)),
            pl.BlockSpec((tm, d), lambda i: (meta(i), 0)),
            pl.BlockSpec((1, d), lambda i: (0, 0)),
            _layer_weight_spec(w_in_b, layer, n_lead),
            pl.BlockSpec(w_tail.shape, lambda i: (0, 0), pipeline_mode=single),
            pl.BlockSpec((d, ns), lambda i: (0, 0), pipeline_mode=single),
        ],
        out_specs=[
            pl.BlockSpec((tm, n), lambda i: (i, 0)),
            pl.BlockSpec((tm, ns), lambda i: (i, 0)),
        ],
        out_shape=[
            jax.ShapeDtypeStruct((m, n), BF16),
            jax.ShapeDtypeStruct((m, ns), F32),
        ],
        compiler_params=pltpu.CompilerParams(
            dimension_semantics=("parallel",),
            vmem_limit_bytes=V7X_VMEM_LIMIT),
        name="in_proj",
    )(h_tok, h_meta, nw, w_in_b, w_tail, w_small)


TAIL = 16
HEAD_REPL = 3


def _conv_silu(raw, tail_ref, c0, c1, shift, w, b):
    x_ext = jnp.concatenate([tail_ref[:, c0:c1], raw], axis=0)
    tail_ref[:, c0:c1] = raw[CHUNK - TAIL:, :]
    delayed = _dot(shift, x_ext)
    acc = b[:, c0:c1] + w[SSM_CONV - 1:SSM_CONV, c0:c1] * raw.astype(F32)
    for s in range(1, SSM_CONV):
        acc = acc + w[SSM_CONV - 1 - s:SSM_CONV - s, c0:c1] * delayed[(s - 1) * CHUNK:s * CHUNK]
    return acc / (1.0 + jnp.exp2(acc * (-LOG2E)))


def _ssd_chunk(sz, x_raw, b_raw, c_raw, dt_raw, shift_ref, cw_ref, cb_ref, dtb_ref, alog_ref, dskip_ref,
               nw_ref, expand_ref, state_ref, tail_ref, y_store, *, meta, n_heads):
    d_inner = x_raw.shape[1]
    gn = b_raw.shape[1]
    gw = d_inner // SSM_GROUPS
    heads_per_group = gw // SSM_HEAD_DIM

    shift = shift_ref[...]
    cw = cw_ref[...]
    cb = cb_ref[...]
    xs = _conv_silu(x_raw, tail_ref, 0, d_inner, shift, cw, cb)
    bm = _conv_silu(b_raw, tail_ref, d_inner, d_inner + gn, shift, cw, cb).astype(BF16)
    cm = _conv_silu(c_raw, tail_ref, d_inner + gn, d_inner + 2 * gn, shift, cw, cb).astype(BF16)

    xdt = dt_raw + dtb_ref[...]
    dt = jnp.maximum(xdt, 0.0) + jnp.log(1.0 + jnp.exp(-jnp.abs(xdt)))
    if meta:
        row = lax.broadcasted_iota(jnp.int32, (CHUNK, 1), 0)
        dt = jnp.where(row >= META_PAD, dt, 0.0)
    adt2 = dt * (-LOG2E * jnp.exp(alog_ref[...]))

    ri = lax.broadcasted_iota(jnp.int32, (CHUNK, CHUNK), 0)
    ci = lax.broadcasted_iota(jnp.int32, (CHUNK, CHUNK), 1)
    causal = ri >= ci
    tri = causal.astype(BF16)
    hi, mid, lo = _split3(adt2)
    a_cs = _dot(tri, hi) + _dot(tri, mid) + _dot(tri, lo)
    row_t = (jnp.log2(dt) - a_cs).T
    exp_a = jnp.exp2(a_cs)
    w_state = jnp.exp2(a_cs[CHUNK - 1:CHUNK] - a_cs) * dt

    grp = lax.broadcasted_iota(jnp.int32, (1, 128), 1) // n_heads
    both = jnp.concatenate([w_state, exp_a], axis=0)
    p0, p1, p2 = _split3(both)
    lhs = jnp.where(grp == 0, p0, jnp.where(grp == 1, p1, p2))
    expanded = _dot(lhs, expand_ref[...])
    w_exp = expanded[0:CHUNK]
    exp_a_exp = expanded[CHUNK:]

    xs_b = xs.astype(BF16)
    xw = (xs * w_exp).astype(BF16)
    lane = lax.broadcasted_iota(jnp.int32, (1, 2 * SSM_HEAD_DIM), 1)
    lo_half = lane < SSM_HEAD_DIM

    def intra(g):
        bm_g = bm[:, g * SSM_STATE:(g + 1) * SSM_STATE]
        cm_g = cm[:, g * SSM_STATE:(g + 1) * SSM_STATE]
        cbm = _dot_nt(cm_g, bm_g).astype(BF16)
        y_pairs = []
        for pj in range(heads_per_group // 2):
            ms = []
            for e in range(2):
                hd = g * heads_per_group + 2 * pj + e
                seg = a_cs[:, hd:hd + 1] + row_t[hd:hd + 1, :]
                decay = jnp.exp2(jnp.where(causal, seg, MASK_NEG))
                ms.append(cbm * decay.astype(BF16))
            lhs_p = jnp.concatenate(ms, axis=1)
            p_off = g * gw + pj * 2 * SSM_HEAD_DIM
            xpair = xs_b[:, p_off:p_off + 2 * SSM_HEAD_DIM]
            zero = jnp.zeros_like(xpair)
            rhs = jnp.concatenate([jnp.where(lo_half, xpair, zero),
                                   jnp.where(lo_half, zero, xpair)], axis=0)
            y_pairs.append(_dot(lhs_p, rhs))
        return jnp.concatenate(y_pairs, axis=1)

    def finish(g, y_diag):
        gs = slice(g * gw, (g + 1) * gw)
        bm_g = bm[:, g * SSM_STATE:(g + 1) * SSM_STATE]
        cm_g = cm[:, g * SSM_STATE:(g + 1) * SSM_STATE]
        prev = state_ref[:, gs]
        y_off = _dot(cm_g, prev.astype(BF16)) * exp_a_exp[:, gs]
        state_ref[:, gs] = prev * exp_a_exp[CHUNK - 1:CHUNK, gs] + _dot_tn(bm_g, xw[:, gs])
        y = (y_diag + y_off + xs[:, gs] * dskip_ref[:, gs]) * sz[:, gs].astype(F32)
        y_store(gs, _rms(y, nw_ref[:, gs]).astype(BF16))

    y_next = intra(0)
    for g in range(SSM_GROUPS):
        y_cur = y_next
        if g + 1 < SSM_GROUPS:
            y_next = intra(g + 1)
        finish(g, y_cur)


def _ssd_body(szt_ref, xt_ref, bt_ref, ct_ref, dtt_ref, szm_ref, xm_ref, bm_ref, cm_ref, dtm_ref,
              shift_ref, cw_ref, cb_ref, dtb_ref, alog_ref, dskip_ref, nw_ref, expand_ref,
              yt_ref, ym_ref, state_ref, tail_ref, *, chunks_per_step, n_heads):
    consts = (shift_ref, cw_ref, cb_ref, dtb_ref, alog_ref, dskip_ref, nw_ref, expand_ref, state_ref, tail_ref)

    @pl.when(pl.program_id(1) == 0)
    def _():
        state_ref[...] = jnp.zeros_like(state_ref)
        tail_ref[...] = jnp.zeros_like(tail_ref)

        def store_meta(cols, val):
            ym_ref[:, cols] = val

        _ssd_chunk(szm_ref[...], xm_ref[...], bm_ref[...], cm_ref[...], dtm_ref[...], *consts,
                   store_meta, meta=True, n_heads=n_heads)

    def body(c, carry):
        rows = pl.ds(pl.multiple_of(c * CHUNK, CHUNK), CHUNK)

        def store_tok(cols, val):
            yt_ref[rows, cols] = val

        _ssd_chunk(szt_ref[rows, :], xt_ref[rows, :], bt_ref[rows, :], ct_ref[rows, :], dtt_ref[rows, :],
                   *consts, store_tok, meta=False, n_heads=n_heads)
        return carry

    lax.fori_loop(0, chunks_per_step, body, 0)


def _ssd(proj, small, shift, conv_w, conv_b, dt_bias, a_log, d_skip, norm_w, expand, *, batch, seq, n_tok, d_inner,
         n_heads, chunks_per_step):
    gn = SSM_GROUPS * SSM_STATE
    conv_dim = d_inner + 2 * gn
    rows = chunks_per_step * CHUNK
    steps = seq // rows
    meta0 = n_tok // CHUNK

    def tok(width_blk):
        return lambda b, s: (b * steps + s, width_blk)

    def meta(width_blk):
        return lambda b, s: (meta0 + b, width_blk)

    const = lambda b, s: (0, 0)
    bc_blk = 2 * d_inner // gn
    return pl.pallas_call(
        functools.partial(_ssd_body, chunks_per_step=chunks_per_step, n_heads=n_heads),
        grid=(batch, steps),
        in_specs=[
            pl.BlockSpec((rows, d_inner), tok(0)),
            pl.BlockSpec((rows, d_inner), tok(1)),
            pl.BlockSpec((rows, gn), tok(bc_blk)),
            pl.BlockSpec((rows, gn), tok(bc_blk + 1)),
            pl.BlockSpec((rows, 128), tok(1)),
            pl.BlockSpec((CHUNK, d_inner), meta(0)),
            pl.BlockSpec((CHUNK, d_inner), meta(1)),
            pl.BlockSpec((CHUNK, gn), meta(bc_blk)),
            pl.BlockSpec((CHUNK, gn), meta(bc_blk + 1)),
            pl.BlockSpec((CHUNK, 128), meta(1)),
            pl.BlockSpec(shift.shape, const),
            pl.BlockSpec((SSM_CONV, conv_dim), const),
            pl.BlockSpec((1, conv_dim), const),
            pl.BlockSpec((1, 128), const),
            pl.BlockSpec((1, 128), const),
            pl.BlockSpec((1, d_inner), const),
            pl.BlockSpec((1, d_inner), const),
            pl.BlockSpec(expand.shape, const),
        ],
        out_specs=[
            pl.BlockSpec((rows, d_inner), tok(0)),
            pl.BlockSpec((CHUNK, d_inner), lambda b, s: (b, 0)),
        ],
        out_shape=[
            jax.ShapeDtypeStruct((n_tok, d_inner), BF16),
            jax.ShapeDtypeStruct((batch * CHUNK, d_inner), BF16),
        ],
        scratch_shapes=[pltpu.VMEM((SSM_STATE, d_inner), F32),
                        pltpu.VMEM((TAIL, conv_dim), BF16)],
        compiler_params=pltpu.CompilerParams(
            dimension_semantics=("parallel", "arbitrary"),
            vmem_limit_bytes=V7X_VMEM_LIMIT),
        name="ssd",
    )(proj, proj, proj, proj, small, proj, proj, proj, proj, small,
      shift, conv_w, conv_b, dt_bias, a_log, d_skip, norm_w, expand)


def _rope(x, cos_t, sin_t):
    return x * cos_t + pltpu.roll(x, 64, 1) * sin_t


def _mla_proj_body(cq_ref, ckv_ref, kr_ref, qnw_ref, kvnw_ref, wq_ref, wk_ref, wvt_ref, cos_ref, sin_ref,
                   q_ref, k_ref, vt_ref, *, q_scale):
    cos_t = cos_ref[...]
    sin_t = sin_ref[...]
    cqn = _rms(cq_ref[...].astype(F32), qnw_ref[...]).astype(BF16)
    qf = _dot(cqn, wq_ref[...])
    ckvn = _rms(ckv_ref[...].astype(F32), kvnw_ref[...]).astype(BF16)
    kf = _dot(ckvn, wk_ref[...])
    vt = _dot_nt(wvt_ref[...], ckvn)
    k_pe = _rope(kr_ref[...], cos_t, sin_t).astype(BF16)
    for hd in range(MLA_HEADS):
        o = hd * QK_SLAB
        q_ref[hd, :, 0:MLA_NOPE] = (qf[:, o:o + MLA_NOPE] * q_scale).astype(BF16)
        q_ref[hd, :, MLA_NOPE:QK_SLAB] = (
            _rope(qf[:, o + MLA_NOPE:o + QK_SLAB], cos_t, sin_t) * q_scale).astype(BF16)
        k_ref[hd, :, 0:MLA_NOPE] = kf[:, hd * MLA_NOPE:(hd + 1) * MLA_NOPE].astype(BF16)
        k_ref[hd, :, MLA_NOPE:QK_SLAB] = k_pe
        vt_ref[hd] = vt[hd * MLA_V:(hd + 1) * MLA_V, :].astype(BF16)


def _mla_proj(proj, small, q_norm_w, kv_norm_w, wq, wk, wvt, cos_tab, sin_tab, *, tm, seq, n_tok,
              cq_col, ckv_col, q_lora, kv_lora):
    m = proj.shape[0]
    n_tok_tiles = n_tok // tm
    tiles_per_seq = seq // tm

    def tab(i):
        return (jnp.where(i < n_tok_tiles, i % tiles_per_seq, tiles_per_seq), 0)

    const = lambda i: (0, 0)
    q_scale = float((MLA_NOPE + MLA_ROPE) ** -0.5 * np.log2(np.e))
    return pl.pallas_call(
        functools.partial(_mla_proj_body, q_scale=q_scale),
        grid=(m // tm,),
        in_specs=[
            pl.BlockSpec((tm, q_lora), lambda i: (i, cq_col // q_lora)),
            pl.BlockSpec((tm, kv_lora), lambda i: (i, ckv_col // kv_lora)),
            pl.BlockSpec((tm, 128), lambda i: (i, 0)),
            pl.BlockSpec((1, q_lora), const),
            pl.BlockSpec((1, kv_lora), const),
            pl.BlockSpec(wq.shape, const),
            pl.BlockSpec(wk.shape, const),
            pl.BlockSpec(wvt.shape, const),
            pl.BlockSpec((tm, 128), tab),
            pl.BlockSpec((tm, 128), tab),
        ],
        out_specs=[
            pl.BlockSpec((MLA_HEADS, tm, QK_SLAB), lambda i: (0, i, 0)),
            pl.BlockSpec((MLA_HEADS, tm, QK_SLAB), lambda i: (0, i, 0)),
            pl.BlockSpec((MLA_HEADS, MLA_V, tm), lambda i: (0, 0, i)),
        ],
        out_shape=[
            jax.ShapeDtypeStruct((MLA_HEADS, m, QK_SLAB), BF16),
            jax.ShapeDtypeStruct((MLA_HEADS, m, QK_SLAB), BF16),
            jax.ShapeDtypeStruct((MLA_HEADS, MLA_V, m), BF16),
        ],
        compiler_params=pltpu.CompilerParams(
            dimension_semantics=("parallel",),
            vmem_limit_bytes=V7X_VMEM_LIMIT),
        name="mla_proj",
    )(proj, proj, small, q_norm_w, kv_norm_w, wq, wk, wvt, cos_tab, sin_tab)


ONES_ROWS = 16


def _softmax_t(s_t, n_keys, masks):
    pieces = []
    pos = 0
    for r0, r1, keep in masks:
        if r0 > pos:
            pieces.append(s_t[pos:r0])
        pieces.append(jnp.where(keep, s_t[r0:r1], MASK_NEG))
        pos = r1
    if pos < n_keys:
        pieces.append(s_t[pos:n_keys])
    m = functools.reduce(jnp.maximum, [jnp.max(p, axis=0, keepdims=True) for p in pieces])
    return jnp.concatenate([jnp.exp2(p - m).astype(BF16) for p in pieces], axis=0)


def _pv_t(p_t, vx_ref, n_keys):
    o_t = _dot(vx_ref[:, 0:n_keys], p_t)
    o_t = o_t[0:MLA_V] * (1.0 / o_t[MLA_V:MLA_V + 1])
    return o_t.T


def _attn_body(qt_ref, qm_ref, kt_ref, km_ref, vt_ref, vm_ref, ot_ref, om_ref, kk_ref, vx_ref, *, tq):
    heads, seq = qt_ref.shape[0], qt_ref.shape[1]
    key_ok = lax.broadcasted_iota(jnp.int32, (CHUNK, 1), 0) >= META_PAD
    rm = lax.broadcasted_iota(jnp.int32, (CHUNK, CHUNK), 0)
    cm = lax.broadcasted_iota(jnp.int32, (CHUNK, CHUNK), 1)
    rq = lax.broadcasted_iota(jnp.int32, (tq, tq), 0)
    cq = lax.broadcasted_iota(jnp.int32, (tq, tq), 1)
    causal_t = rq <= cq

    blocks = []
    for hh in range(heads):
        kk_ref[hh, 0:CHUNK, :] = km_ref[hh]
        kk_ref[hh, CHUNK:, :] = kt_ref[hh]
        vx_ref[hh, 0:MLA_V, 0:CHUNK] = vm_ref[hh]
        vx_ref[hh, 0:MLA_V, CHUNK:] = vt_ref[hh]
        vx_ref[hh, MLA_V:, :] = jnp.ones((ONES_ROWS, seq + CHUNK), BF16)
        meta_block = (hh, qm_ref, 0, CHUNK, CHUNK, [(0, CHUNK, (rm <= cm) & (rm >= META_PAD))], om_ref)
        tok_blocks = []
        for qs in range(0, seq, tq):
            n_keys = CHUNK + qs + tq
            tok_blocks.append((hh, qt_ref, qs, tq, n_keys,
                               [(0, CHUNK, key_ok), (n_keys - tq, n_keys, causal_t)], ot_ref))
        blocks += tok_blocks[:1] + tok_blocks[:0:-1] + [meta_block]

    def scores(blk):
        hh, q_ref, qs, rows, n_keys, _, _ = blk
        return _dot_nt(kk_ref[hh, 0:n_keys, :], q_ref[hh, qs:qs + rows, :])

    def finish(blk, p_t):
        hh, _, qs, rows, n_keys, _, o_ref = blk
        o_ref[hh, qs:qs + rows, :] = _pv_t(p_t, vx_ref.at[hh], n_keys).astype(o_ref.dtype)

    s_next = scores(blocks[0])
    p_prev = None
    for bi, blk in enumerate(blocks):
        s_cur = s_next
        if bi + 1 < len(blocks):
            s_next = scores(blocks[bi + 1])
        if p_prev is not None:
            finish(blocks[bi - 1], p_prev)
        p_prev = _softmax_t(s_cur, blk[4], blk[5])
    finish(blocks[-1], p_prev)


def _attention(q, k, vt, *, batch, seq, n_tok, tq, heads_per_step):
    lp = seq + CHUNK
    meta0 = n_tok // CHUNK
    hps = heads_per_step
    tok_rows = lambda b, h: (h, b, 0)
    meta_rows = lambda b, h: (h, meta0 + b, 0)
    return pl.pallas_call(
        functools.partial(_attn_body, tq=tq),
        grid=(batch, MLA_HEADS // hps),
        in_specs=[
            pl.BlockSpec((hps, seq, QK_SLAB), tok_rows),
            pl.BlockSpec((hps, CHUNK, QK_SLAB), meta_rows),
            pl.BlockSpec((hps, seq, QK_SLAB), tok_rows),
            pl.BlockSpec((hps, CHUNK, QK_SLAB), meta_rows),
            pl.BlockSpec((hps, MLA_V, seq), lambda b, h: (h, 0, b)),
            pl.BlockSpec((hps, MLA_V, CHUNK), lambda b, h: (h, 0, meta0 + b)),
        ],
        out_specs=[
            pl.BlockSpec((hps, seq, MLA_V), tok_rows),
            pl.BlockSpec((hps, CHUNK, MLA_V), tok_rows),
        ],
        out_shape=[
            jax.ShapeDtypeStruct((MLA_HEADS, n_tok, MLA_V), BF16),
            jax.ShapeDtypeStruct((MLA_HEADS, batch * CHUNK, MLA_V), BF16),
        ],
        scratch_shapes=[pltpu.VMEM((hps, lp, QK_SLAB), BF16),
                        pltpu.VMEM((hps, MLA_V + ONES_ROWS, lp), BF16)],
        compiler_params=pltpu.CompilerParams(
            dimension_semantics=("parallel", "parallel"),
            vmem_limit_bytes=V7X_VMEM_LIMIT),
        name="mla_attention",
    )(q, q, k, k, vt, vt)


def _mix_body(yst_ref, ysm_ref, ymt_ref, ymm_ref, gs_ref, gm_ref, ht_ref, hm_ref, wbs_ref, wbm_ref, wo_ref,
              o_ref, *, n_tok_tiles):
    i = pl.program_id(0)
    tm = o_ref.shape[0]
    is_meta = i >= n_tok_tiles
    ymt = jnp.concatenate([ymt_ref[hd] for hd in range(MLA_HEADS)], axis=1)
    ymm = jnp.concatenate([ymm_ref[hd] for hd in range(MLA_HEADS)], axis=1)
    ym = jnp.where(is_meta, ymm, ymt)
    h = jnp.where(is_meta, hm_ref[...], ht_ref[...])
    ys = jnp.where(is_meta, ysm_ref[...], yst_ref[...])
    a = _dot(ys, wbs_ref[...])
    b = _dot(ym, wbm_ref[...])
    mixed = _sigmoid(gs_ref[...].astype(F32)) * a + _sigmoid(gm_ref[...].astype(F32)) * b
    hn = h + _dot(mixed.astype(BF16), wo_ref[...])
    row = lax.broadcasted_iota(jnp.int32, (tm, 1), 0)
    inert = is_meta & ((row & (CHUNK - 1)) < META_PAD)
    o_ref[...] = jnp.where(inert, 0.0, hn)


def _mix(y_ssm_tok, y_ssm_meta, y_mla_tok, y_mla_meta, proj, h_tok, h_meta, w_bs, w_bm, w_o, layer, *, tm, n_tok,
         n_meta, rows, meta_base_rows, gs_col, gm_col):
    d = h_tok.shape[1]
    m = rows
    n_tok_tiles, tok, meta = _split_rows(n_tok, n_meta, tm, meta_base_rows)
    _, _, meta0 = _split_rows(n_tok, n_meta, tm, 0)
    return pl.pallas_call(
        functools.partial(_mix_body, n_tok_tiles=n_tok_tiles),
        grid=(m // tm,),
        in_specs=[
            pl.BlockSpec((tm, y_ssm_tok.shape[1]), lambda i: (tok(i), 0)),
            pl.BlockSpec((tm, y_ssm_tok.shape[1]), lambda i: (meta0(i), 0)),
            pl.BlockSpec((MLA_HEADS, tm, MLA_V), lambda i: (0, tok(i), 0)),
            pl.BlockSpec((MLA_HEADS, tm, MLA_V), lambda i: (0, meta0(i), 0)),
            pl.BlockSpec((tm, d), lambda i: (i, gs_col // d)),
            pl.BlockSpec((tm, d), lambda i: (i, gm_col // d)),
            pl.BlockSpec((tm, d), lambda i: (tok(i), 0)),
            pl.BlockSpec((tm, d), lambda i: (meta(i), 0)),
            _layer_weight_spec(w_bs, layer),
            _layer_weight_spec(w_bm, layer),
            _layer_weight_spec(w_o, layer),
        ],
        out_specs=pl.BlockSpec((tm, d), lambda i: (i, 0)),
        out_shape=jax.ShapeDtypeStruct((m, d), F32),
        compiler_params=pltpu.CompilerParams(
            dimension_semantics=("parallel",),
            vmem_limit_bytes=V7X_VMEM_LIMIT),
        name="mix_out",
    )(y_ssm_tok, y_ssm_meta, y_mla_tok, y_mla_meta, proj, proj, h_tok, h_meta, w_bs, w_bm, w_o)


def _mlp_body(h_ref, nw_ref, wu_ref, wd_ref, fnw_ref, o_ref, *, ff_chunk, final):
    h = h_ref[...]
    v = _rms(h, nw_ref[...]).astype(BF16)
    d_ff = wu_ref.shape[1]
    acc = jnp.zeros_like(h)
    for f0 in range(0, d_ff, ff_chunk):
        a = _dot(v, wu_ref[:, f0:f0 + ff_chunk])
        a = jnp.square(jnp.maximum(a, 0.0)).astype(BF16)
        acc = acc + _dot(a, wd_ref[f0:f0 + ff_chunk, :])
    hn = h + acc
    if final:
        hn = _rms(hn, fnw_ref[...])
    o_ref[...] = hn


def _mlp(h, nw, w_up, w_down, layer, final_nw, *, tm, rows, final):
    d = h.shape[1]
    const = lambda i: (0, 0)
    return pl.pallas_call(
        functools.partial(_mlp_body, ff_chunk=1024, final=final),
        grid=(rows // tm,),
        in_specs=[
            pl.BlockSpec((tm, d), lambda i: (i, 0)),
            pl.BlockSpec((1, d), const),
            _layer_weight_spec(w_up, layer),
            _layer_weight_spec(w_down, layer),
            pl.BlockSpec((1, d), const),
        ],
        out_specs=pl.BlockSpec((tm, d), lambda i: (i, 0)),
        out_shape=jax.ShapeDtypeStruct((rows, d), F32),
        compiler_params=pltpu.CompilerParams(
            dimension_semantics=("parallel",),
            vmem_limit_bytes=V7X_VMEM_LIMIT),
        name="mlp_final" if final else "mlp",
    )(h, nw, w_up, w_down, final_nw)


def _rope_tables(seq, tm):
    half = MLA_ROPE // 2
    f32 = np.float32
    inv = np.power(f32(ROPE_THETA), -np.arange(0, MLA_ROPE, 2, dtype=f32) / f32(MLA_ROPE)).astype(f32)
    tok_pos = np.arange(N_META, N_META + seq, dtype=f32)
    meta_pos = np.maximum(np.arange(CHUNK, dtype=f32) - f32(META_PAD), f32(0.0))
    pos = np.concatenate([tok_pos, np.tile(meta_pos, tm // CHUNK)])
    ang = (pos[:, None] * inv[None, :]).astype(f32)
    cos, sin = np.cos(ang).astype(f32), np.sin(ang).astype(f32)
    zero = np.zeros_like(cos)
    assert 4 * half == 128
    return (jnp.asarray(np.concatenate([cos, zero, cos, zero], axis=1)),
            jnp.asarray(np.concatenate([-sin, zero, sin, zero], axis=1)))


def _spread_rope_cols(w):
    half = MLA_ROPE // 2
    zero = jnp.zeros(w.shape[:-1] + (half,), w.dtype)
    return jnp.concatenate([w[..., :half], zero, w[..., half:], zero], axis=-1)


def kernel(x, meta_tokens, norm_mix_w, w_in, conv_w, conv_b, dt_bias, a_log, d_skip, ssm_norm_w,
           q_norm_w, kv_norm_w, w_uq, w_ukv, w_branch_ssm, w_branch_mla, w_out, norm_mlp_w,
           w_mlp_up, w_mlp_down, final_norm_w):
    batch, seq, d = x.shape
    depth = w_in.shape[0]
    d_inner = w_branch_ssm.shape[1]
    n_heads = dt_bias.shape[1]
    q_lora = q_norm_w.shape[1]
    kv_lora = kv_norm_w.shape[1]
    gn = SSM_GROUPS * SSM_STATE
    conv_dim = d_inner + 2 * gn
    n_tok = batch * seq
    n_meta_rows = batch * CHUNK
    assert d_inner == n_heads * SSM_HEAD_DIM and conv_w.shape[2] == conv_dim
    assert n_heads <= 128 and (d_inner // SSM_GROUPS // SSM_HEAD_DIM) % 2 == 0

    tm = min(1024, n_meta_rows)
    tq = 256
    assert seq % tm == 0 and n_meta_rows % tm == 0 and tm % CHUNK == 0 and seq % tq == 0

    meta_chunk = jnp.concatenate([jnp.zeros((META_PAD, d), x.dtype), meta_tokens.astype(x.dtype)], axis=0)
    h_tok = x.reshape(n_tok, d)
    h_meta = jnp.tile(meta_chunk, (batch, 1))
    meta_base_rows = 0

    o_z = 0
    o_xbc = o_z + d_inner
    o_dt = o_xbc + conv_dim
    o_cq = o_dt + n_heads
    o_ckv = o_cq + q_lora
    o_kr = o_ckv + kv_lora
    o_gs = o_kr + MLA_ROPE
    o_gm = o_gs + d
    assert o_gm + d == w_in.shape[2]
    cq_col = d_inner + conv_dim
    ckv_col = cq_col + q_lora
    gs_col = -(-(ckv_col + kv_lora) // d) * d
    gm_col = gs_col + d
    n_main = gm_col + d
    tn = 1024
    assert n_main % tn == 0 and d_inner % tn == 0 and cq_col % q_lora == 0 and ckv_col % kv_lora == 0

    tm_rope = min(512, tm)
    cos_tab, sin_tab = _rope_tables(seq, tm)
    assert HEAD_REPL * n_heads <= 128
    head_of_lane = np.arange(d_inner) // SSM_HEAD_DIM
    src_lane = np.arange(128)
    expand = ((src_lane % n_heads)[:, None] == head_of_lane[None, :]) & (src_lane < HEAD_REPL * n_heads)[:, None]
    expand = jnp.asarray(expand, BF16)
    rr = np.arange((SSM_CONV - 1) * CHUNK)
    shift = jnp.asarray(np.arange(TAIL + CHUNK)[None, :] == (TAIL + rr % CHUNK - (rr // CHUNK + 1))[:, None], BF16)

    assert o_dt % tn == 0 and cq_col == o_dt
    w_lead_b = w_in[:, :, :o_dt].astype(BF16)
    w_rest_b = w_in[:, :, o_dt:].astype(BF16)
    w_bs_b = w_branch_ssm.astype(BF16)
    w_bm_b = w_branch_mla.astype(BF16)
    w_out_b = w_out.astype(BF16)
    w_up_b = w_mlp_up.astype(BF16)
    w_down_b = w_mlp_down.astype(BF16)

    out = None
    for i in range(depth):
        wr = w_rest_b[i]
        w_tail = jnp.concatenate([
            wr[:, o_cq - o_dt:o_kr - o_dt], jnp.zeros((d, gs_col - ckv_col - kv_lora), wr.dtype),
            wr[:, o_gs - o_dt:]], axis=1)
        w_small = jnp.concatenate([
            _spread_rope_cols(wr[:, o_kr - o_dt:o_gs - o_dt])] + [wr[:, 0:o_cq - o_dt]] * HEAD_REPL + [
            jnp.zeros((d, 128 - HEAD_REPL * n_heads), wr.dtype)], axis=1)
        dtb = jnp.pad(jnp.tile(dt_bias[i], HEAD_REPL), (0, 128 - HEAD_REPL * n_heads))[None]
        alog = jnp.pad(jnp.tile(a_log[i], HEAD_REPL), (0, 128 - HEAD_REPL * n_heads))[None]
        dskip = jnp.repeat(d_skip[i], SSM_HEAD_DIM)[None]
        wq = w_uq[i].reshape(q_lora, MLA_HEADS, MLA_NOPE + MLA_ROPE)
        wq = jnp.concatenate([wq[..., :MLA_NOPE], _spread_rope_cols(wq[..., MLA_NOPE:])], axis=-1)
        wq = wq.reshape(q_lora, MLA_HEADS * QK_SLAB).astype(BF16)
        wkv = w_ukv[i].reshape(kv_lora, MLA_HEADS, MLA_NOPE + MLA_V)
        wk = wkv[..., :MLA_NOPE].reshape(kv_lora, MLA_HEADS * MLA_NOPE).astype(BF16)
        wvt = wkv[..., MLA_NOPE:].reshape(kv_lora, MLA_HEADS * MLA_V).T.astype(BF16)

        proj, small = _in_proj(h_tok, h_meta, norm_mix_w[i][None], w_lead_b, i, o_dt, w_tail, w_small, tm=tm_rope,
                               tn=tn, n_tok=n_tok, n_meta=n_meta_rows, meta_base_rows=meta_base_rows,
                               silu_cols=d_inner)
        ys_tok, ys_meta = _ssd(proj, small, shift, conv_w[i].astype(F32), conv_b[i][None], dtb, alog, dskip,
                               ssm_norm_w[i][None], expand, batch=batch, seq=seq, n_tok=n_tok, d_inner=d_inner,
                               n_heads=n_heads, chunks_per_step=min(8, seq // CHUNK))
        q, k, vt = _mla_proj(proj, small, q_norm_w[i][None], kv_norm_w[i][None], wq, wk, wvt, cos_tab, sin_tab,
                             tm=tm, seq=seq, n_tok=n_tok, cq_col=cq_col, ckv_col=ckv_col,
                             q_lora=q_lora, kv_lora=kv_lora)
        y_tok, y_meta = _attention(q, k, vt, batch=batch, seq=seq, n_tok=n_tok, tq=tq, heads_per_step=4)
        last = i == depth - 1
        h = _mix(ys_tok, ys_meta, y_tok, y_meta, proj, h_tok, h_meta, w_bs_b, w_bm_b, w_out_b, i, tm=tm_rope,
                 n_tok=n_tok, n_meta=n_meta_rows, rows=n_tok if last else n_tok + n_meta_rows,
                 meta_base_rows=meta_base_rows, gs_col=gs_col, gm_col=gm_col)
        if last:
            out = _mlp(h, norm_mlp_w[i][None], w_up_b, w_down_b, i, final_norm_w[None], tm=tm, rows=n_tok,
                       final=True)
        else:
            h = _mlp(h, norm_mlp_w[i][None], w_up_b, w_down_b, i, final_norm_w[None], tm=tm, rows=h.shape[0],
                     final=False)
            h_tok, h_meta, meta_base_rows = h, h, n_tok
    return out.reshape(batch, seq, d)
```
